```python
import functools
import jax, jax.numpy as jnp
from jax import lax
import numpy as np

D_MODEL = 2048
BATCH = 2
SEQ = 4096
DEPTH = 1
DEC_BATCH = 32
DEC_SEQ = 1
PAST_LEN = 16384
PAGE_SIZE = 128

HEAD_DIM = 64
N_RWKV_HEADS = 16
N_FOX_HEADS = 16
RWKV_WIDTH = N_RWKV_HEADS * HEAD_DIM
FOX_WIDTH = N_FOX_HEADS * HEAD_DIM
MIX_WIDTH = RWKV_WIDTH + FOX_WIDTH
DECAY_LORA = 64
AAA_LORA = 64
GATE_LORA = 128
RWKV_COLS = 3 * RWKV_WIDTH + DECAY_LORA + AAA_LORA + GATE_LORA
FOX_COLS = 4 * FOX_WIDTH + N_FOX_HEADS
IN_COLS = RWKV_COLS + FOX_COLS
SPLIT_RWKV = (RWKV_WIDTH, 2 * RWKV_WIDTH, 3 * RWKV_WIDTH, 3 * RWKV_WIDTH + DECAY_LORA,
              3 * RWKV_WIDTH + DECAY_LORA + AAA_LORA)
SPLIT_FOX = (FOX_WIDTH, 2 * FOX_WIDTH, 3 * FOX_WIDTH, 4 * FOX_WIDTH)
D_FF = -(-8 * D_MODEL // (3 * 256)) * 256
Q_BLOCK = 128
NORM_EPS = 1e-6
GN_EPS = 64e-5
ATTN_SCALE = HEAD_DIM ** -0.5

kernel_name = "hymba_rwkv7_fox_decode_step"


def rmsnorm(x, g):
    xf = x.astype(jnp.float32)
    y = xf * lax.rsqrt(jnp.mean(xf * xf, axis=-1, keepdims=True) + NORM_EPS)
    return (y * g.astype(jnp.float32)).astype(x.dtype)


def wkv_step(S, inp):
    r, decay, k, v, kk, a = inp
    sa = jnp.einsum('bhij,bhj->bhi', S, -kk)
    S = (S * decay[:, :, None, :] + sa[..., None] * (kk * a)[:, :, None, :]
         + v[..., None] * k[:, :, None, :])
    return S, jnp.einsum('bhij,bhj->bhi', S, r)


def rwkv_time_mix(z, z_prev, S0, mu_shift, w0, w_decay_up, a0, w_aaa_up, w_gate_up,
                  k_k, k_a, r_k, lnx_g, lnx_b):
    f32 = jnp.float32
    B, T, _ = z.shape
    H, N = N_RWKV_HEADS, HEAD_DIM
    zs = z + (z_prev - z) * mu_shift
    r, k, v, xw, xa, xg = jnp.split(zs, SPLIT_RWKV, axis=-1)
    wlog = -jax.nn.softplus(-(w0 + jnp.tanh(xw) @ w_decay_up).astype(f32)) - 0.5
    decay = jnp.exp(-jnp.exp(wlog))
    a = jax.nn.sigmoid((a0 + xa @ w_aaa_up).astype(f32))
    g = jax.nn.sigmoid(xg) @ w_gate_up
    hs = lambda t: t.reshape(B, T, H, N).astype(f32)
    r, k, v, a, decay = hs(r), hs(k), hs(v), hs(a), hs(decay)
    kk = k * k_k.reshape(H, N).astype(f32)
    kk = kk / jnp.maximum(jnp.sqrt(jnp.sum(kk * kk, axis=-1, keepdims=True)), 1e-12)
    k = k * (1.0 + (a - 1.0) * k_a.reshape(H, N).astype(f32))
    xs = tuple(t.swapaxes(0, 1) for t in (r, decay, k, v, kk, a))
    S_final, o = lax.scan(wkv_step, S0.astype(f32), xs)
    o = o.swapaxes(0, 1)
    mean = jnp.mean(o, axis=-1, keepdims=True)
    var = jnp.mean(jnp.square(o - mean), axis=-1, keepdims=True)
    o = ((o - mean) * lax.rsqrt(var + GN_EPS)).reshape(B, T, RWKV_WIDTH)
    o = o * lnx_g.astype(f32) + lnx_b.astype(f32)
    bonus = jnp.sum(r * k * r_k.astype(f32), axis=-1, keepdims=True) * v
    o = (o + bonus.reshape(B, T, RWKV_WIDTH)) * g.astype(f32)
    return o.astype(z.dtype), S_final.astype(S0.dtype)


def fox_project(z, b_f, q_norm, k_norm):
    B, T, _ = z.shape
    q, k, v, og, fl = jnp.split(z, SPLIT_FOX, axis=-1)
    hs = lambda t: t.reshape(B, T, N_FOX_HEADS, HEAD_DIM)
    q = rmsnorm(hs(q), q_norm)
    k = rmsnorm(hs(k), k_norm)
    logf = jax.nn.log_sigmoid((fl + b_f).astype(jnp.float32))
    return q, k, hs(v), og, logf


def fox_block(q, c_q, pos_q, k, v, c_k, pos_k):
    s = jnp.einsum('bqhd,bkhd->bhqk', q, k).astype(jnp.float32) * ATTN_SCALE
    s = s + (c_q.transpose(0, 2, 1)[..., :, None] - c_k.transpose(0, 2, 1)[..., None, :])
    s = jnp.where(pos_k[None, :] <= pos_q[:, None], s, -jnp.inf)
    p = jax.nn.softmax(s, axis=-1).astype(v.dtype)
    return jnp.einsum('bhqk,bkhd->bqhd', p, v)


def fox_prompt(q, k, v, logf):
    B, T, H, D = q.shape
    nb = T // Q_BLOCK
    c = jnp.cumsum(logf, axis=1)
    pos = jnp.arange(T, dtype=jnp.int32)
    qb = q.reshape(B, nb, Q_BLOCK, H, D).swapaxes(0, 1)
    cb = c.reshape(B, nb, Q_BLOCK, H).swapaxes(0, 1)
    pb = pos.reshape(nb, Q_BLOCK)
    ob = lax.map(lambda t: fox_block(t[0], t[1], t[2], k, v, c, pos), (qb, cb, pb))
    return ob.swapaxes(0, 1).reshape(B, T, H, D)


def fox_sample(q, k, v, logf, cache_k, cache_v, cache_logf, page_table, layer):
    DB, T, H, D = q.shape
    past = page_table.shape[1] * cache_k.shape[2]
    k_past = cache_k[layer, page_table].reshape(DB, past, H, D)
    v_past = cache_v[layer, page_table].reshape(DB, past, H, D)
    lf_past = cache_logf[layer, page_table].reshape(DB, past, H).astype(jnp.float32)
    k_all = jnp.concatenate([k_past, k.astype(k_past.dtype)], axis=1)
    v_all = jnp.concatenate([v_past, v.astype(v_past.dtype)], axis=1)
    c = jnp.cumsum(jnp.concatenate([lf_past, logf], axis=1), axis=1)
    pos = jnp.arange(past + T, dtype=jnp.int32)
    return fox_block(q.astype(k_all.dtype), c[:, past:], pos[past:], k_all, v_all, c, pos)


def hybrid_layer(x, shift_row, S0, attend, p):
    B, T, _ = x.shape
    xn = rmsnorm(x, p['norm_mix'])
    z = xn @ p['w_in']
    z_rwkv, z_fox = z[..., :RWKV_COLS], z[..., RWKV_COLS:]
    z_prev = jnp.concatenate([shift_row[:, None, :].astype(z.dtype), z_rwkv[:, :-1]], axis=1)
    o_rwkv, S_new = rwkv_time_mix(z_rwkv, z_prev, S0, p['mu_shift'], p['w0'], p['w_decay_up'],
                                  p['a0'], p['w_aaa_up'], p['w_gate_up'], p['k_k'], p['k_a'],
                                  p['r_k'], p['lnx_g'], p['lnx_b'])
    q, k, v, og, logf = fox_project(z_fox, p['b_f'], p['q_norm'], p['k_norm'])
    o_fox = attend(q, k, v, logf).reshape(B, T, FOX_WIDTH).astype(x.dtype) * jax.nn.sigmoid(og)
    h = x + jnp.concatenate([o_rwkv.astype(x.dtype), o_fox], axis=-1) @ p['w_out']
    hn = rmsnorm(h, p['norm_ffn'])
    h = h + (jax.nn.silu(hn @ p['w_ffn_gate']) * (hn @ p['w_ffn_up'])) @ p['w_ffn_down']
    return h, (k, v, logf, S_new, z_rwkv[:, -1])


def setup_inputs(seed: int = 0) -> dict:
    key = jax.random.key(seed)
    ks = jax.random.split(key, 32)
    f32 = jnp.float32
    nrm = lambda k, shape, s: jax.random.normal(k, shape, f32) * s
    n_pages = PAST_LEN // PAGE_SIZE
    n_used = DEC_BATCH * n_pages
    n_pool = n_used + max(1, n_used // 4)
    perm = jax.random.permutation(ks[0], n_pool)
    page_table = perm[:n_used].reshape(DEC_BATCH, n_pages).astype(jnp.int32)
    L = DEPTH
    return {
        'x_prompt': nrm(ks[1], (BATCH, SEQ, D_MODEL), 1.0),
        'x_sample': nrm(ks[2], (DEC_BATCH, DEC_SEQ, D_MODEL), 1.0),
        'cache_k': nrm(ks[3], (L, n_pool, PAGE_SIZE, N_FOX_HEADS, HEAD_DIM), 1.0),
        'cache_v': nrm(ks[4], (L, n_pool, PAGE_SIZE, N_FOX_HEADS, HEAD_DIM), 1.0),
        'cache_logf': jax.nn.log_sigmoid(3.0 + nrm(ks[5], (L, n_pool, PAGE_SIZE, N_FOX_HEADS), 1.0)),
        'state_wkv': nrm(ks[6], (L, DEC_BATCH, N_RWKV_HEADS, HEAD_DIM, HEAD_DIM), 0.3),
        'state_shift': nrm(ks[7], (L, DEC_BATCH, RWKV_COLS), 1.0),
        'page_table': page_table,
        'norm_mix': 1.0 + nrm(ks[8], (L, D_MODEL), 0.02),
        'w_in': nrm(ks[9], (L, D_MODEL, IN_COLS), D_MODEL ** -0.5),
        'mu_shift': jax.random.uniform(ks[10], (L, RWKV_COLS), f32),
        'w0': jax.random.uniform(ks[11], (L, RWKV_WIDTH), f32, -6.0, 1.0),
        'w_decay_up': nrm(ks[12], (L, DECAY_LORA, RWKV_WIDTH), 0.1),
        'a0': nrm(ks[13], (L, RWKV_WIDTH), 0.1),
        'w_aaa_up': nrm(ks[14], (L, AAA_LORA, RWKV_WIDTH), 0.5 * AAA_LORA ** -0.5),
        'w_gate_up': nrm(ks[15], (L, GATE_LORA, RWKV_WIDTH), GATE_LORA ** -0.5),
        'k_k': 0.85 + nrm(ks[16], (L, RWKV_WIDTH), 0.02),
        'k_a': 1.0 + nrm(ks[17], (L, RWKV_WIDTH), 0.02),
        'r_k': nrm(ks[18], (L, N_RWKV_HEADS, HEAD_DIM), 0.1),
        'lnx_g': 1.0 + nrm(ks[19], (L, RWKV_WIDTH), 0.02),
        'lnx_b': nrm(ks[20], (L, RWKV_WIDTH), 0.02),
        'b_f': jax.random.uniform(ks[21], (L, N_FOX_HEADS), f32, 1.0, 5.0),
        'q_norm': 1.0 + nrm(ks[22], (L, HEAD_DIM), 0.02),
        'k_norm': 1.0 + nrm(ks[23], (L, HEAD_DIM), 0.02),
        'w_out': nrm(ks[24], (L, MIX_WIDTH, D_MODEL), MIX_WIDTH ** -0.5),
        'norm_ffn': 1.0 + nrm(ks[25], (L, D_MODEL), 0.02),
        'w_ffn_gate': nrm(ks[26], (L, D_MODEL, D_FF), D_MODEL ** -0.5),
        'w_ffn_up': nrm(ks[27], (L, D_MODEL, D_FF), D_MODEL ** -0.5),
        'w_ffn_down': nrm(ks[28], (L, D_FF, D_MODEL), D_FF ** -0.5),
        'norm_final': 1.0 + nrm(ks[29], (D_MODEL,), 0.02),
    }


def reference(x_prompt, x_sample, cache_k, cache_v, cache_logf, state_wkv, state_shift, page_table,
              norm_mix, w_in, mu_shift, w0, w_decay_up, a0, w_aaa_up, w_gate_up, k_k, k_a, r_k,
              lnx_g, lnx_b, b_f, q_norm, k_norm, w_out, norm_ffn, w_ffn_gate, w_ffn_up, w_ffn_down,
              norm_final):
    B = x_prompt.shape[0]
    h_p, h_s = x_prompt, x_sample
    st_p, st_s = [], []
    for l in range(DEPTH):
        p = dict(norm_mix=norm_mix[l], w_in=w_in[l], mu_shift=mu_shift[l], w0=w0[l],
                 w_decay_up=w_decay_up[l], a0=a0[l], w_aaa_up=w_aaa_up[l], w_gate_up=w_gate_up[l],
                 k_k=k_k[l], k_a=k_a[l], r_k=r_k[l], lnx_g=lnx_g[l], lnx_b=lnx_b[l], b_f=b_f[l],
                 q_norm=q_norm[l], k_norm=k_norm[l], w_out=w_out[l], norm_ffn=norm_ffn[l],
                 w_ffn_gate=w_ffn_gate[l], w_ffn_up=w_ffn_up[l], w_ffn_down=w_ffn_down[l])
        shift0 = jnp.zeros((B, RWKV_COLS), x_prompt.dtype)
        S0 = jnp.zeros((B, N_RWKV_HEADS, HEAD_DIM, HEAD_DIM), state_wkv.dtype)
        h_p, sp = hybrid_layer(h_p, shift0, S0, fox_prompt, p)
        attend_s = functools.partial(fox_sample, cache_k=cache_k, cache_v=cache_v,
                                     cache_logf=cache_logf, page_table=page_table, layer=l)
        h_s, ss = hybrid_layer(h_s, state_shift[l], state_wkv[l], attend_s, p)
        st_p.append(sp)
        st_s.append(ss)
    y_prompt = rmsnorm(h_p, norm_final)
    y_sample = rmsnorm(h_s, norm_final)
    k_prompt = jnp.stack([s[0] for s in st_p])
    v_prompt = jnp.stack([s[1] for s in st_p])
    logf_prompt = jnp.stack([s[2] for s in st_p])
    wkv_prompt = jnp.stack([s[3] for s in st_p])
    shift_prompt = jnp.stack([s[4] for s in st_p])
    k_sample = jnp.stack([s[0] for s in st_s])
    v_sample = jnp.stack([s[1] for s in st_s])
    logf_sample = jnp.stack([s[2] for s in st_s])
    wkv_sample = jnp.stack([s[3] for s in st_s])
    shift_sample = jnp.stack([s[4] for s in st_s])
    return (y_prompt, y_sample, k_prompt, v_prompt, logf_prompt, wkv_prompt, shift_prompt,
            k_sample, v_sample, logf_sample, wkv_sample, shift_sample)
```

```python
import functools

import jax
import jax.numpy as jnp
from jax import lax
from jax.experimental import pallas as pl
from jax.experimental.pallas import tpu as pltpu

F32 = jnp.float32
BF16 = jnp.bfloat16

HEAD_DIM = 64
N_HEADS = 16
WIDTH = N_HEADS * HEAD_DIM
N_PAIRS = N_HEADS // 2
LANES = 128
LORA_COLS = 256
RWKV_COLS = 3 * WIDTH + LORA_COLS
NORM_EPS = 1e-6
GN_EPS = 64e-5
ATTN_SCALE = HEAD_DIM ** -0.5
EXP_NEG_HALF = 0.6065306597126334
NEG_BIG = -1e30

Z_COLS = 7680
COL_FQ, COL_FK, COL_FV, COL_OG = 0, 1024, 2048, 3072
COL_R, COL_K, COL_V, COL_L = 4096, 5120, 6144, 7168
COL_FL = 7424
N_Z_USED = 7440

VMEM_LIMIT = 56 * 1024 * 1024


def _cparams(sem):
    return pltpu.CompilerParams(dimension_semantics=sem, vmem_limit_bytes=VMEM_LIMIT)


def _sigmoid(x):
    return 1.0 / (1.0 + jnp.exp(-x))


def _dot(a, b):
    return jnp.dot(a, b, preferred_element_type=F32)


def _split2(x):
    hi = x.astype(BF16)
    lo = (x - hi.astype(F32)).astype(BF16)
    return hi, lo


def _split3(x):
    h1 = x.astype(BF16)
    r1 = x - h1.astype(F32)
    h2 = r1.astype(BF16)
    h3 = (r1 - h2.astype(F32)).astype(BF16)
    return h1, h2, h3


def _seg_sum(x, bd):
    parts = []
    for c in range(x.shape[1] // LANES):
        hi, lo = _split2(x[:, c * LANES:(c + 1) * LANES])
        parts.append(_dot(hi, bd) + _dot(lo, bd))
    return parts[0] if len(parts) == 1 else jnp.concatenate(parts, axis=1)


def _norm_matmul_kernel(x_ref, g_ref, w_ref, o_ref, xn_ref):
    @pl.when(pl.program_id(1) == 0)
    def _():
        x = x_ref[...]
        ms = jnp.mean(x * x, axis=-1, keepdims=True)
        xn_ref[...] = (x * lax.rsqrt(ms + NORM_EPS) * g_ref[...]).astype(BF16)

    o_ref[...] = _dot(xn_ref[...], w_ref[...])


def _norm_matmul(x, g, w, tm, tn):
    m, d = x.shape
    n = w.shape[1]
    return pl.pallas_call(
        _norm_matmul_kernel,
        out_shape=jax.ShapeDtypeStruct((m, n), F32),
        grid=(m // tm, n // tn),
        in_specs=[
            pl.BlockSpec((tm, d), lambda i, j: (i, 0)),
            pl.BlockSpec((1, d), lambda i, j: (0, 0)),
            pl.BlockSpec((d, tn), lambda i, j: (0, j)),
        ],
        out_specs=pl.BlockSpec((tm, tn), lambda i, j: (i, j)),
        scratch_shapes=[pltpu.VMEM((tm, d), BF16)],
        compiler_params=_cparams(("parallel", "arbitrary")),
        name="in_proj",
    )(x, g, w)


def _pre_math(zr, zk, zv, zl, pr, pk, pv, plo, mu_ref, w0_ref, a0_ref, kk_ref, ka_ref, rk_ref,
              wda_ref, wg_ref, bd_ref):
    bd = bd_ref[...]
    r = zr + (pr - zr) * mu_ref[:, 0:WIDTH]
    k = zk + (pk - zk) * mu_ref[:, WIDTH:2 * WIDTH]
    v = zv + (pv - zv) * mu_ref[:, 2 * WIDTH:3 * WIDTH]
    lo = zl + (plo - zl) * mu_ref[:, 3 * WIDTH:3 * WIDTH + LORA_COLS]
    l0 = lo[:, 0:LANES]
    lane = lax.broadcasted_iota(jnp.int32, l0.shape, 1)
    lhs = jnp.where(lane < HEAD_DIM, jnp.tanh(l0), l0).astype(BF16)
    da = _dot(lhs, wda_ref[...])
    decay = jnp.exp(-EXP_NEG_HALF * _sigmoid(w0_ref[...] + da[:, 0:WIDTH]))
    a = _sigmoid(a0_ref[...] + da[:, WIDTH:2 * WIDTH])
    g = _dot(_sigmoid(lo[:, LANES:2 * LANES]).astype(BF16), wg_ref[...])
    kk = k * kk_ref[...]
    nrm = jnp.sqrt(_seg_sum(kk * kk, bd))
    kkn = kk / jnp.maximum(nrm, 1e-12)
    k2 = k * (1.0 + (a - 1.0) * ka_ref[...])
    b = kkn * a
    bonus = _seg_sum(r * k2 * rk_ref[...], bd) * v
    v_hi = v.astype(BF16).astype(F32)
    v_lo = (v - v_hi).astype(BF16).astype(F32)
    return r, decay, k2, v_hi, v_lo, kkn, b, bonus, g


def _shift_rows(x, carry_ref):
    n = x.shape[0]
    prev0 = carry_ref[...]
    if n == 1:
        prev = prev0
    else:
        rolled = pltpu.roll(x, 1, 0)
        row = lax.broadcasted_iota(jnp.int32, x.shape, 0)
        prev = jnp.where(row == 0, prev0, rolled)
    carry_ref[...] = x[n - 1:n, :]
    return prev


def _pre_carry_kernel(zr_ref, zk_ref, zv_ref, zl_ref, fr_ref, fk_ref, fv_ref, fl_ref,
                      mu_ref, w0_ref, a0_ref, kk_ref, ka_ref, rk_ref, wda_ref, wg_ref, bd_ref,
                      r_o, w_o, k_o, vh_o, vl_o, kn_o, b_o, bo_o, g_o,
                      cr_ref, ck_ref, cv_ref, cl_ref):
    first = pl.program_id(1) == 0

    @pl.when(first)
    def _():
        cr_ref[...] = fr_ref[0]
        ck_ref[...] = fk_ref[0]
        cv_ref[...] = fv_ref[0]
        cl_ref[...] = fl_ref[0]

    zr, zk, zv, zl = zr_ref[0], zk_ref[0], zv_ref[0], zl_ref[0]
    pr = _shift_rows(zr, cr_ref)
    pk = _shift_rows(zk, ck_ref)
    pv = _shift_rows(zv, cv_ref)
    plo = _shift_rows(zl, cl_ref)
    outs = _pre_math(zr, zk, zv, zl, pr, pk, pv, plo, mu_ref, w0_ref, a0_ref, kk_ref, ka_ref,
                     rk_ref, wda_ref, wg_ref, bd_ref)
    for o_ref, val in zip((r_o, w_o, k_o, vh_o, vl_o, kn_o, b_o, bo_o, g_o), outs):
        o_ref[0] = val


def _pre_given_kernel(zr_ref, zk_ref, zv_ref, zl_ref, pr_ref, pk_ref, pv_ref, pl_ref,
                      mu_ref, w0_ref, a0_ref, kk_ref, ka_ref, rk_ref, wda_ref, wg_ref, bd_ref,
                      r_o, w_o, k_o, vh_o, vl_o, kn_o, b_o, bo_o, g_o):
    outs = _pre_math(zr_ref[0], zk_ref[0], zv_ref[0], zl_ref[0],
                     pr_ref[0], pk_ref[0], pv_ref[0], pl_ref[0],
                     mu_ref, w0_ref, a0_ref, kk_ref, ka_ref, rk_ref, wda_ref, wg_ref, bd_ref)
    for o_ref, val in zip((r_o, w_o, k_o, vh_o, vl_o, kn_o, b_o, bo_o, g_o), outs):
        o_ref[0] = val


def _rwkv_prelude(z3, prev3, params, tb, carry):
    bsz, t, _ = z3.shape
    nt = t // tb
    z_specs = [
        pl.BlockSpec((1, tb, WIDTH), lambda b, i: (b, i, COL_R // WIDTH)),
        pl.BlockSpec((1, tb, WIDTH), lambda b, i: (b, i, COL_K // WIDTH)),
        pl.BlockSpec((1, tb, WIDTH), lambda b, i: (b, i, COL_V // WIDTH)),
        pl.BlockSpec((1, tb, LORA_COLS), lambda b, i: (b, i, COL_L // LORA_COLS)),
    ]
    pt = 1 if carry else tb
    pidx = (lambda b, i: 0) if carry else (lambda b, i: i)
    p_specs = [
        pl.BlockSpec((1, pt, WIDTH), lambda b, i: (b, pidx(b, i), 0)),
        pl.BlockSpec((1, pt, WIDTH), lambda b, i: (b, pidx(b, i), 1)),
        pl.BlockSpec((1, pt, WIDTH), lambda b, i: (b, pidx(b, i), 2)),
        pl.BlockSpec((1, pt, LORA_COLS), lambda b, i: (b, pidx(b, i), 3 * WIDTH // LORA_COLS)),
    ]
    const = lambda shape: pl.BlockSpec(shape, lambda b, i: (0,) * len(shape))
    w_specs = [const((1, RWKV_COLS)), const((1, WIDTH)), const((1, WIDTH)), const((1, WIDTH)),
               const((1, WIDTH)), const((1, WIDTH)), const((LANES, 2 * WIDTH)),
               const((LANES, WIDTH)), const((LANES, LANES))]
    out_spec = pl.BlockSpec((1, tb, WIDTH), lambda b, i: (b, i, 0))
    out_shape = [jax.ShapeDtypeStruct((bsz, t, WIDTH), F32)] * 9
    scratch = ([pltpu.VMEM((1, WIDTH), F32)] * 3 + [pltpu.VMEM((1, LORA_COLS), F32)]) if carry else []
    return pl.pallas_call(
        _pre_carry_kernel if carry else _pre_given_kernel,
        out_shape=out_shape,
        grid=(bsz, nt),
        in_specs=z_specs + p_specs + w_specs,
        out_specs=[out_spec] * 9,
        scratch_shapes=scratch,
        compiler_params=_cparams(("parallel", "arbitrary")),
        name="rwkv_prelude",
    )(z3, z3, z3, z3, prev3, prev3, prev3, prev3, *params)


def _scan_kernel(r_ref, w_ref, k_ref, vh_ref, vl_ref, kn_ref, b_ref, s0_ref, mask_ref, bd_ref,
                 bd2_ref, o_ref, sout_ref, s_ref, *, bb, tt):
    ti = pl.program_id(1)

    @pl.when(ti == 0)
    def _():
        for bi in range(bb):
            for p in range(N_PAIRS):
                s_ref[bi * N_PAIRS + p] = s0_ref[bi, p]

    maskf = mask_ref[...]
    bd = bd_ref[...]
    bd2 = bd2_ref[...]

    def step(t, carry):
        for bi in range(bb):
            rows = [ref[bi, pl.ds(t, 1), :] for ref in (r_ref, w_ref, k_ref, vh_ref, vl_ref, kn_ref, b_ref)]
            o_rows = []
            for p in range(N_PAIRS):
                g = bi * N_PAIRS + p
                r_t, w_t, k_t, vh_t, vl_t, kn_t, b_t = [x[:, p * LANES:(p + 1) * LANES] for x in rows]
                s = s_ref[g]
                hi, lo = _split2(s * kn_t)
                sa = _dot(jnp.concatenate([hi, lo], axis=1), bd2)
                vt = _dot(jnp.concatenate([(maskf * vh_t).astype(BF16),
                                           (maskf * vl_t).astype(BF16)], axis=1), bd2)
                s_new = s * w_t - sa * b_t + vt * k_t
                s_ref[g] = s_new
                of = _dot((s_new * r_t).astype(BF16), bd)
                o_rows.append(jnp.sum(maskf * of, axis=0, keepdims=True))
            o_ref[bi, pl.ds(t, 1), :] = jnp.concatenate(o_rows, axis=1)
        return carry

    lax.fori_loop(0, tt, step, 0)

    @pl.when(ti == pl.num_programs(1) - 1)
    def _():
        for bi in range(bb):
            for p in range(N_PAIRS):
                sout_ref[bi, p] = s_ref[bi * N_PAIRS + p]


def _rwkv_scan(r, w, k, vh, vl, kn, b, s0, consts, tt):
    bsz, t, _ = r.shape
    bb = 2
    seq_spec = pl.BlockSpec((bb, tt, WIDTH), lambda bi, ti: (bi, ti, 0))
    st_spec = pl.BlockSpec((bb, N_PAIRS, HEAD_DIM, LANES), lambda bi, ti: (bi, 0, 0, 0))
    const = lambda shape: pl.BlockSpec(shape, lambda bi, ti: (0,) * len(shape))
    return pl.pallas_call(
        functools.partial(_scan_kernel, bb=bb, tt=tt),
        out_shape=[jax.ShapeDtypeStruct((bsz, t, WIDTH), F32),
                   jax.ShapeDtypeStruct((bsz, N_PAIRS, HEAD_DIM, LANES), F32)],
        grid=(bsz // bb, t // tt),
        in_specs=[seq_spec] * 7 + [st_spec, const((HEAD_DIM, LANES)), const((LANES, LANES)),
                                   const((2 * LANES, LANES))],
        out_specs=[seq_spec, st_spec],
        scratch_shapes=[pltpu.VMEM((bb * N_PAIRS, HEAD_DIM, LANES), F32)],
        compiler_params=_cparams(("parallel", "arbitrary")),
        name="rwkv_scan",
    )(r, w, k, vh, vl, kn, b, s0, *consts)


def _pairs_from_heads(s):
    bsz = s.shape[0]
    s = s.reshape(bsz, N_PAIRS, 2, HEAD_DIM, HEAD_DIM).transpose(0, 1, 3, 2, 4)
    return s.reshape(bsz, N_PAIRS, HEAD_DIM, LANES)


def _heads_from_pairs(s):
    bsz = s.shape[0]
    s = s.reshape(bsz, N_PAIRS, HEAD_DIM, 2, HEAD_DIM).transpose(0, 1, 3, 2, 4)
    return s.reshape(bsz, N_HEADS, HEAD_DIM, HEAD_DIM)


def _post_kernel(o_ref, bo_ref, g_ref, lg_ref, lb_ref, bd_ref, out_ref):
    bd = bd_ref[...]
    o = o_ref[0]
    mean = _seg_sum(o, bd) * (1.0 / HEAD_DIM)
    d = o - mean
    var = _seg_sum(d * d, bd) * (1.0 / HEAD_DIM)
    on = d * lax.rsqrt(var + GN_EPS) * lg_ref[...] + lb_ref[...]
    out_ref[0] = ((on + bo_ref[0]) * g_ref[0]).astype(BF16)


def _rwkv_post(o, bonus, g, lnx_g, lnx_b, bd, tb):
    bsz, t, _ = o.shape
    spec = pl.BlockSpec((1, tb, WIDTH), lambda b, i: (b, i, 0))
    const = lambda shape: pl.BlockSpec(shape, lambda b, i: (0,) * len(shape))
    return pl.pallas_call(
        _post_kernel,
        out_shape=jax.ShapeDtypeStruct((bsz, t, WIDTH), BF16),
        grid=(bsz, t // tb),
        in_specs=[spec, spec, spec, const((1, WIDTH)), const((1, WIDTH)), const((LANES, LANES))],
        out_specs=spec,
        compiler_params=_cparams(("parallel", "parallel")),
        name="rwkv_post",
    )(o, bonus, g, lnx_g, lnx_b, bd)


def _fox_norm(x, gain, bd):
    ms = _seg_sum(x * x, bd) * (1.0 / HEAD_DIM)
    return x * lax.rsqrt(ms + NORM_EPS) * gain


def _log_sigmoid(x):
    return jnp.minimum(x, 0.0) - jnp.log(1.0 + jnp.exp(-jnp.abs(x)))


def _fox_prep_kernel(fq_ref, fk_ref, fv_ref, fl_ref, bf_ref, qg_ref, kg_ref, bd_ref, tril_ref,
                     pq_ref, pk_ref, kn_o, v_o, lf_o, qa_o, ka_o, va_o, c_ref):
    bd = bd_ref[...]
    q = _fox_norm(fq_ref[0], qg_ref[...], bd) * ATTN_SCALE
    kn = _fox_norm(fk_ref[0], kg_ref[...], bd)
    v = fv_ref[0]
    kn_o[0] = kn
    v_o[0] = v
    lane = lax.broadcasted_iota(jnp.int32, fl_ref[0].shape, 1)
    lf = jnp.where(lane < N_HEADS, _log_sigmoid(fl_ref[0] + bf_ref[...]), 0.0)
    lf_o[0] = lf

    @pl.when(pl.program_id(1) == 0)
    def _():
        c_ref[...] = jnp.zeros_like(c_ref)

    tril = tril_ref[...]
    h1, h2, h3 = _split3(lf)
    c = _dot(tril, h1) + _dot(tril, h2) + _dot(tril, h3) + c_ref[...]
    n = c.shape[0]
    c_ref[...] = c[n - 1:n, :]
    c1, c2, c3 = _split3(c)
    cpack = (c1.astype(F32) + pltpu.roll(c2.astype(F32), N_HEADS, 1)
             + pltpu.roll(c3.astype(F32), 2 * N_HEADS, 1)).astype(BF16)
    lane_v = lax.broadcasted_iota(jnp.int32, (n, LANES), 1)
    ones_q = jnp.where((lane_v >= HEAD_DIM + 3) & (lane_v < HEAD_DIM + 6), 1.0, 0.0)
    ones_k = jnp.where((lane_v >= HEAD_DIM) & (lane_v < HEAD_DIM + 3), 1.0, 0.0)
    for p in range(N_PAIRS):
        sl = slice(p * LANES, (p + 1) * LANES)
        lhs_q = jnp.concatenate([q[:, sl].astype(BF16), cpack], axis=1)
        lhs_k = jnp.concatenate([kn[:, sl].astype(BF16), cpack], axis=1)
        vs = v[:, sl]
        for hh in range(2):
            h = 2 * p + hh
            qa_o[0, h] = (_dot(lhs_q, pq_ref[h]) + ones_q).astype(BF16)
            ka_o[0, h] = (_dot(lhs_k, pk_ref[h]) + ones_k).astype(BF16)
            if hh == 0:
                va = jnp.where(lane_v < HEAD_DIM, vs, jnp.where(lane_v == HEAD_DIM, 1.0, 0.0))
            else:
                va = jnp.where(lane_v >= HEAD_DIM, vs, jnp.where(lane_v == 0, 1.0, 0.0))
            va_o[0, h] = va.astype(BF16)


def _fox_prep_plain_kernel(fq_ref, fk_ref, fv_ref, fl_ref, bf_ref, qg_ref, kg_ref, bd_ref,
                           kn_o, v_o, lf_o, q_o):
    bd = bd_ref[...]
    q_o[0] = _fox_norm(fq_ref[0], qg_ref[...], bd) * ATTN_SCALE
    kn_o[0] = _fox_norm(fk_ref[0], kg_ref[...], bd)
    v_o[0] = fv_ref[0]
    lane = lax.broadcasted_iota(jnp.int32, fl_ref[0].shape, 1)
    lf_o[0] = jnp.where(lane < N_HEADS, _log_sigmoid(fl_ref[0] + bf_ref[...]), 0.0)


def _fox_prep(z3, params, tb, augment):
    bsz, t, _ = z3.shape
    z_specs = [
        pl.BlockSpec((1, tb, WIDTH), lambda b, i: (b, i, COL_FQ // WIDTH)),
        pl.BlockSpec((1, tb, WIDTH), lambda b, i: (b, i, COL_FK // WIDTH)),
        pl.BlockSpec((1, tb, WIDTH), lambda b, i: (b, i, COL_FV // WIDTH)),
        pl.BlockSpec((1, tb, LANES), lambda b, i: (b, i, COL_FL // LANES)),
    ]
    const = lambda shape: pl.BlockSpec(shape, lambda b, i: (0,) * len(shape))
    w_specs = [const((1, LANES)), const((1, WIDTH)), const((1, WIDTH)), const((LANES, LANES))]
    row_spec = pl.BlockSpec((1, tb, WIDTH), lambda b, i: (b, i, 0))
    lf_spec = pl.BlockSpec((1, tb, LANES), lambda b, i: (b, i, 0))
    row_shape = jax.ShapeDtypeStruct((bsz, t, WIDTH), F32)
    lf_shape = jax.ShapeDtypeStruct((bsz, t, LANES), F32)
    if not augment:
        return pl.pallas_call(
            _fox_prep_plain_kernel,
            out_shape=[row_shape, row_shape, lf_shape, row_shape],
            grid=(bsz, t // tb),
            in_specs=z_specs + w_specs,
            out_specs=[row_spec, row_spec, lf_spec, row_spec],
            compiler_params=_cparams(("parallel", "arbitrary")),
            name="fox_prep_sample",
        )(z3, z3, z3, z3, *params[:4])
    aug_spec = pl.BlockSpec((1, N_HEADS, tb, LANES), lambda b, i: (b, 0, i, 0))
    aug_shape = jax.ShapeDtypeStruct((bsz, N_HEADS, t, LANES), BF16)
    w_specs += [const((tb, tb)), const((N_HEADS, 2 * LANES, LANES)),
                const((N_HEADS, 2 * LANES, LANES))]
    return pl.pallas_call(
        _fox_prep_kernel,
        out_shape=[row_shape, row_shape, lf_shape, aug_shape, aug_shape, aug_shape],
        grid=(bsz, t // tb),
        in_specs=z_specs + w_specs,
        out_specs=[row_spec, row_spec, lf_spec, aug_spec, aug_spec, aug_spec],
        scratch_shapes=[pltpu.VMEM((1, LANES), F32)],
        compiler_params=_cparams(("parallel", "arbitrary")),
        name="fox_prep",
    )(z3, z3, z3, z3, *params)


def _place_matrices():
    h = jnp.arange(N_HEADS)[:, None, None]
    row = jnp.arange(2 * LANES)[None, :, None]
    col = jnp.arange(LANES)[None, None, :]
    data = (row < LANES) & (row == (h % 2) * HEAD_DIM + col) & (col < HEAD_DIM)
    piece = (row - LANES) // N_HEADS
    is_c = (row >= LANES) & (row < LANES + 3 * N_HEADS) & ((row - LANES) % N_HEADS == h)
    cq = is_c & (col == HEAD_DIM + piece)
    ck = is_c & (col == HEAD_DIM + 3 + piece)
    pq = (data | cq).astype(F32)
    pk = data.astype(F32) - ck.astype(F32)
    return pq.astype(BF16), pk.astype(BF16)


def _flash_kernel(q_ref, k_ref, v_ref, og_ref, o_ref, *, tq, tk):
    qi = pl.program_id(2)
    row = lax.broadcasted_iota(jnp.int32, (tq, tk), 0)
    col = lax.broadcasted_iota(jnp.int32, (tq, tk), 1)
    causal = col <= row
    outs = []
    for hh in range(2):
        q = q_ref[0, hh]

        def block(j, m, acc, masked, hh=hh, q=q):
            start = pl.multiple_of(j * tk, tk)
            kb = k_ref[0, hh, pl.ds(start, tk), :]
            vb = v_ref[0, hh, pl.ds(start, tk), :]
            s = lax.dot_general(q, kb, (((1,), (1,)), ((), ())), preferred_element_type=F32)
            if masked:
                s = jnp.where(causal, s, NEG_BIG)
            m_new = jnp.maximum(m, jnp.max(s, axis=1, keepdims=True))
            alpha = jnp.exp(m - m_new)
            p = jnp.exp(s - m_new)
            acc = alpha * acc + _dot(p.astype(BF16), vb)
            return m_new, acc

        m0 = jnp.full((tq, 1), NEG_BIG, F32)
        a0 = jnp.zeros((tq, LANES), F32)
        m, acc = lax.fori_loop(0, qi, lambda j, c: block(j, c[0], c[1], False), (m0, a0))
        m, acc = block(qi, m, acc, True)
        l_lane = HEAD_DIM if hh == 0 else 0
        outs.append(acc / acc[:, l_lane:l_lane + 1])
    lane = lax.broadcasted_iota(jnp.int32, (tq, LANES), 1)
    o = jnp.where(lane < HEAD_DIM, outs[0], outs[1])
    o_ref[0] = (o * _sigmoid(og_ref[0])).astype(BF16)


def _fox_flash(qa, ka, va, z3, tq):
    bsz, _, t, _ = qa.shape
    tk = tq
    return pl.pallas_call(
        functools.partial(_flash_kernel, tq=tq, tk=tk),
        out_shape=jax.ShapeDtypeStruct((bsz, t, WIDTH), BF16),
        grid=(bsz, N_PAIRS, t // tq),
        in_specs=[
            pl.BlockSpec((1, 2, tq, LANES), lambda b, p, i: (b, p, i, 0)),
            pl.BlockSpec((1, 2, t, LANES), lambda b, p, i: (b, p, 0, 0)),
            pl.BlockSpec((1, 2, t, LANES), lambda b, p, i: (b, p, 0, 0)),
            pl.BlockSpec((1, tq, LANES), lambda b, p, i: (b, i, COL_OG // LANES + p)),
        ],
        out_specs=pl.BlockSpec((1, tq, LANES), lambda b, p, i: (b, i, p)),
        compiler_params=_cparams(("parallel", "parallel", "arbitrary")),
        name="fox_flash",
    )(qa, ka, va, z3)


def _decode_kernel(pt_ref, qx_ref, kc_ref, vc_ref, lc_ref, og_ref, us_ref, ex_ref, *rest, npg):
    k_refs = rest[0:npg]
    v_refs = rest[npg:2 * npg]
    l_refs = rest[2 * npg:3 * npg]
    o_ref = rest[3 * npg]
    m_ref, l_ref, cy_ref, acc_ref, pad_ref = rest[3 * npg + 1:]
    step = pl.program_id(1)
    qx = qx_ref[0]
    ex = ex_ref[...]
    us = us_ref[...]
    page = k_refs[0].shape[1]

    def expand(rowvec):
        hi, lo = _split2(jnp.broadcast_to(rowvec, (8, LANES)))
        return _dot(hi, ex) + _dot(lo, ex)

    @pl.when(step == 0)
    def _():
        kc = jnp.broadcast_to(kc_ref[0], (8, WIDTH)).astype(BF16)
        m_ref[...] = _dot(kc, qx)[0:1, :]
        l_ref[...] = jnp.ones_like(l_ref)
        cy_ref[...] = lc_ref[0]
        rowi = lax.broadcasted_iota(jnp.int32, (8, WIDTH), 0)
        acc_ref[...] = jnp.where(rowi == 0, jnp.broadcast_to(vc_ref[0], (8, WIDTH)), 0.0)
        pad_ref[...] = jnp.zeros_like(pad_ref)

    for i in range(npg):
        pad_ref[:, 0:N_HEADS] = l_refs[i][0]
        lf = pad_ref[...]
        h1, h2, h3 = _split3(lf)
        bias = _dot(us, h1) + _dot(us, h2) + _dot(us, h3) + cy_ref[...]
        cy_ref[...] = bias[0:1, :] + lf[0:1, :]
        s = _dot(k_refs[i][0].astype(BF16), qx) + bias
        m_old = m_ref[...]
        m_new = jnp.maximum(m_old, jnp.max(s, axis=0, keepdims=True))
        alpha = jnp.exp(m_old - m_new)
        p = jnp.exp(s - m_new)
        m_ref[...] = m_new
        l_ref[...] = alpha * l_ref[...] + jnp.sum(p, axis=0, keepdims=True)
        pv = _dot(p.astype(BF16), ex) * v_refs[i][0]
        part = pv[0:8, :]
        for r in range(1, page // 8):
            part = part + pv[8 * r:8 * r + 8, :]
        acc_ref[...] = expand(alpha) * acc_ref[...] + part

    @pl.when(step == pl.num_programs(1) - 1)
    def _():
        num = jnp.sum(acc_ref[...], axis=0, keepdims=True)
        den = expand(l_ref[...])[0:1, :]
        o_ref[0] = ((num / den) * _sigmoid(og_ref[0])).astype(BF16)


def _fox_decode(page_table, qx, k_cur, v_cur, lf_cur, z3s, cache_k2, cache_v2, cache_lf, us, ex, npg):
    dbsz, n_pages = page_table.shape
    page = cache_k2.shape[1]
    steps = n_pages // npg

    def page_map(i):
        return lambda b, s, pt: (pt[b, n_pages - 1 - (s * npg + i)], 0, 0)

    row = lambda b, s, pt: (b, 0, 0)
    in_specs = [
        pl.BlockSpec((1, WIDTH, LANES), row),
        pl.BlockSpec((1, 1, WIDTH), row),
        pl.BlockSpec((1, 1, WIDTH), row),
        pl.BlockSpec((1, 1, LANES), row),
        pl.BlockSpec((1, 1, WIDTH), lambda b, s, pt: (b, 0, COL_OG // WIDTH)),
        pl.BlockSpec((page, page), lambda b, s, pt: (0, 0)),
        pl.BlockSpec((LANES, WIDTH), lambda b, s, pt: (0, 0)),
    ]
    in_specs += [pl.BlockSpec((1, page, WIDTH), page_map(i)) for i in range(npg)]
    in_specs += [pl.BlockSpec((1, page, WIDTH), page_map(i)) for i in range(npg)]
    in_specs += [pl.BlockSpec((1, page, N_HEADS), page_map(i)) for i in range(npg)]
    grid_spec = pltpu.PrefetchScalarGridSpec(
        num_scalar_prefetch=1,
        grid=(dbsz, steps),
        in_specs=in_specs,
        out_specs=pl.BlockSpec((1, 1, WIDTH), row),
        scratch_shapes=[pltpu.VMEM((1, LANES), F32), pltpu.VMEM((1, LANES), F32),
                        pltpu.VMEM((1, LANES), F32), pltpu.VMEM((8, WIDTH), F32),
                        pltpu.VMEM((page, LANES), F32)],
    )
    return pl.pallas_call(
        functools.partial(_decode_kernel, npg=npg),
        out_shape=jax.ShapeDtypeStruct((dbsz, 1, WIDTH), BF16),
        grid_spec=grid_spec,
        compiler_params=_cparams(("parallel", "arbitrary")),
        name="fox_decode",
    )(page_table, qx, k_cur, v_cur, lf_cur, z3s, us, ex,
      *([cache_k2] * npg), *([cache_v2] * npg), *([cache_lf] * npg))


def _out_proj_kernel(x_ref, oa_ref, ob_ref, wa_ref, wb_ref, h_ref):
    h_ref[...] = x_ref[...] + _dot(oa_ref[...], wa_ref[...]) + _dot(ob_ref[...], wb_ref[...])


def _out_proj(x, oa, ob, w_out, tm, tn):
    m, d = x.shape
    return pl.pallas_call(
        _out_proj_kernel,
        out_shape=jax.ShapeDtypeStruct((m, d), F32),
        grid=(m // tm, d // tn),
        in_specs=[
            pl.BlockSpec((tm, tn), lambda i, j: (i, j)),
            pl.BlockSpec((tm, WIDTH), lambda i, j: (i, 0)),
            pl.BlockSpec((tm, WIDTH), lambda i, j: (i, 0)),
            pl.BlockSpec((WIDTH, tn), lambda i, j: (0, j)),
            pl.BlockSpec((WIDTH, tn), lambda i, j: (1, j)),
        ],
        out_specs=pl.BlockSpec((tm, tn), lambda i, j: (i, j)),
        compiler_params=_cparams(("parallel", "arbitrary")),
        name="out_proj",
    )(x, oa, ob, w_out, w_out)


def _ffn_kernel(h_ref, gn_ref, wg_ref, wu_ref, wd_ref, gf_ref, y_ref, hn_ref, acc_ref):
    f = pl.program_id(1)

    @pl.when(f == 0)
    def _():
        h = h_ref[...]
        ms = jnp.mean(h * h, axis=-1, keepdims=True)
        hn_ref[...] = (h * lax.rsqrt(ms + NORM_EPS) * gn_ref[...]).astype(BF16)
        acc_ref[...] = jnp.zeros_like(acc_ref)

    hn = hn_ref[...]
    gate = _dot(hn, wg_ref[...])
    up = _dot(hn, wu_ref[...])
    act = (gate * _sigmoid(gate) * up).astype(BF16)
    acc_ref[...] += _dot(act, wd_ref[...])

    @pl.when(f == pl.num_programs(1) - 1)
    def _():
        hh = h_ref[...] + acc_ref[...]
        ms = jnp.mean(hh * hh, axis=-1, keepdims=True)
        y_ref[...] = hh * lax.rsqrt(ms + NORM_EPS) * gf_ref[...]


def _ffn(h, g_ffn, wg, wu, wd, g_final, tm, tf):
    m, d = h.shape
    dff = wg.shape[1]
    return pl.pallas_call(
        _ffn_kernel,
        out_shape=jax.ShapeDtypeStruct((m, d), F32),
        grid=(m // tm, dff // tf),
        in_specs=[
            pl.BlockSpec((tm, d), lambda i, f: (i, 0)),
            pl.BlockSpec((1, d), lambda i, f: (0, 0)),
            pl.BlockSpec((d, tf), lambda i, f: (0, f)),
            pl.BlockSpec((d, tf), lambda i, f: (0, f)),
            pl.BlockSpec((tf, d), lambda i, f: (f, 0)),
            pl.BlockSpec((1, d), lambda i, f: (0, 0)),
        ],
        out_specs=pl.BlockSpec((tm, d), lambda i, f: (i, 0)),
        scratch_shapes=[pltpu.VMEM((tm, d), BF16), pltpu.VMEM((tm, d), F32)],
        compiler_params=_cparams(("parallel", "arbitrary")),
        name="ffn",
    )(h, g_ffn, wg, wu, wd, g_final)


def _block_diag_ones():
    i = jnp.arange(LANES)
    return (i[:, None] // HEAD_DIM == i[None, :] // HEAD_DIM).astype(BF16)


def _pick(n, cap):
    t = min(n, cap)
    while n % t:
        t //= 2
    return t


def kernel(x_prompt, x_sample, cache_k, cache_v, cache_logf, state_wkv, state_shift, page_table,
           norm_mix, w_in, mu_shift, w0, w_decay_up, a0, w_aaa_up, w_gate_up, k_k, k_a, r_k,
           lnx_g, lnx_b, b_f, q_norm, k_norm, w_out, norm_ffn, w_ffn_gate, w_ffn_up, w_ffn_down,
           norm_final):
    bsz, seq, d = x_prompt.shape
    dbsz = x_sample.shape[0]
    lyr = 0
    rw = RWKV_COLS

    wi = w_in[lyr]
    w_z = jnp.concatenate(
        [wi[:, rw:rw + 4 * WIDTH], wi[:, :rw], wi[:, rw + 4 * WIDTH:],
         jnp.zeros((d, Z_COLS - N_Z_USED), F32)], axis=1).astype(BF16)
    g_mix = norm_mix[lyr][None, :]
    wda = jnp.zeros((LANES, 2 * WIDTH), F32)
    wda = wda.at[:HEAD_DIM, :WIDTH].set(w_decay_up[lyr]).at[HEAD_DIM:, WIDTH:].set(w_aaa_up[lyr])
    bd = _block_diag_ones()
    bd2 = jnp.concatenate([bd, bd], axis=0)
    pre_params = (mu_shift[lyr][None, :], w0[lyr][None, :], a0[lyr][None, :], k_k[lyr][None, :],
                  k_a[lyr][None, :], r_k[lyr].reshape(1, WIDTH), wda.astype(BF16),
                  w_gate_up[lyr].astype(BF16), bd)
    li = jnp.arange(LANES)
    maskf = (li[None, :] % HEAD_DIM == jnp.arange(HEAD_DIM)[:, None]).astype(F32)
    scan_consts = (maskf, bd, bd2)
    lg, lb = lnx_g[lyr][None, :], lnx_b[lyr][None, :]
    bfp = jnp.zeros((1, LANES), F32).at[0, :N_HEADS].set(b_f[lyr])
    qg = jnp.tile(q_norm[lyr], N_HEADS)[None, :]
    kg = jnp.tile(k_norm[lyr], N_HEADS)[None, :]
    tb_fox = _pick(seq, 256)
    tril = (jnp.arange(tb_fox)[:, None] >= jnp.arange(tb_fox)[None, :]).astype(BF16)
    pq, pk = _place_matrices()
    w_o = w_out[lyr].astype(BF16)
    wg = w_ffn_gate[lyr].astype(BF16)
    wu = w_ffn_up[lyr].astype(BF16)
    wd = w_ffn_down[lyr].astype(BF16)
    g_ffn = norm_ffn[lyr][None, :]
    g_fin = norm_final[None, :]

    m_p = bsz * seq
    xp = x_prompt.reshape(m_p, d)
    z_p = _norm_matmul(xp, g_mix, w_z, _pick(m_p, 1024), 512)
    z3p = z_p.reshape(bsz, seq, Z_COLS)
    first = jnp.zeros((bsz, 1, rw), F32)
    r, wdec, k2, vh, vl, kn, bb_, bonus, gate = _rwkv_prelude(z3p, first, pre_params,
                                                             _pick(seq, 256), True)
    s0 = jnp.zeros((bsz, N_PAIRS, HEAD_DIM, LANES), F32)
    o_p, s_p = _rwkv_scan(r, wdec, k2, vh, vl, kn, bb_, s0, scan_consts, _pick(seq, 128))
    oa_p = _rwkv_post(o_p, bonus, gate, lg, lb, bd, _pick(seq, 512))
    fkn_p, fv_p, lf_p, qa, ka, va = _fox_prep(z3p, (bfp, qg, kg, bd, tril, pq, pk), tb_fox, True)
    ob_p = _fox_flash(qa, ka, va, z3p, _pick(seq, 256))
    h_p = _out_proj(xp, oa_p.reshape(m_p, WIDTH), ob_p.reshape(m_p, WIDTH), w_o,
                    _pick(m_p, 1024), 512)
    y_p = _ffn(h_p, g_ffn, wg, wu, wd, g_fin, _pick(m_p, 512), 512)

    xs = x_sample.reshape(dbsz, d)
    z_s = _norm_matmul(xs, g_mix, w_z, dbsz, 512)
    z3s = z_s.reshape(1, dbsz, Z_COLS)
    prev_s = state_shift[lyr].reshape(1, dbsz, rw)
    outs = _rwkv_prelude(z3s, prev_s, pre_params, dbsz, False)
    r, wdec, k2, vh, vl, kn, bb_, bonus, gate = [a.reshape(dbsz, 1, WIDTH) for a in outs]
    o_s, s_s = _rwkv_scan(r, wdec, k2, vh, vl, kn, bb_, _pairs_from_heads(state_wkv[lyr]),
                          scan_consts, 1)
    oa_s = _rwkv_post(o_s.reshape(1, dbsz, WIDTH), bonus.reshape(1, dbsz, WIDTH),
                      gate.reshape(1, dbsz, WIDTH), lg, lb, bd, dbsz)
    fkn_s, fv_s, lf_s, q_s = _fox_prep(z3s, (bfp, qg, kg, bd), dbsz, False)
    eye = jnp.eye(N_HEADS, LANES, dtype=F32)
    qx = (q_s.reshape(dbsz, N_HEADS, HEAD_DIM, 1) * eye[None, :, None, :]).reshape(
        dbsz, WIDTH, LANES).astype(BF16)
    n_pool, page = cache_k.shape[1], cache_k.shape[2]
    us = (jnp.arange(page)[None, :] > jnp.arange(page)[:, None]).astype(BF16)
    ex = (jnp.arange(LANES)[:, None] == jnp.arange(WIDTH)[None, :] // HEAD_DIM).astype(BF16)
    ob_s = _fox_decode(page_table, qx, fkn_s.reshape(dbsz, 1, WIDTH), fv_s.reshape(dbsz, 1, WIDTH),
                       lf_s.reshape(dbsz, 1, LANES), z_s.reshape(dbsz, 1, Z_COLS),
                       cache_k[lyr].reshape(n_pool, page, WIDTH),
                       cache_v[lyr].reshape(n_pool, page, WIDTH), cache_logf[lyr], us, ex,
                       _pick(page_table.shape[1], 8))
    h_s = _out_proj(xs, oa_s.reshape(dbsz, WIDTH), ob_s.reshape(dbsz, WIDTH), w_o, dbsz, 512)
    y_s = _ffn(h_s, g_ffn, wg, wu, wd, g_fin, dbsz, 512)

    hd = (N_HEADS, HEAD_DIM)
    return (
        y_p.reshape(bsz, seq, d),
        y_s.reshape(dbsz, 1, d),
        fkn_p.reshape(1, bsz, seq, *hd),
        fv_p.reshape(1, bsz, seq, *hd),
        lf_p[:, :, :N_HEADS].reshape(1, bsz, seq, N_HEADS),
        _heads_from_pairs(s_p)[None],
        z3p[:, seq - 1, COL_R:COL_R + rw][None],
        fkn_s.reshape(1, dbsz, 1, *hd),
        fv_s.reshape(1, dbsz, 1, *hd),
        lf_s[0, :, :N_HEADS].reshape(1, dbsz, 1, N_HEADS),
        _heads_from_pairs(s_s)[None],
        z_s[:, COL_R:COL_R + rw][None],
    )
```

```python
import functools

import jax
import jax.numpy as jnp
from jax import lax
from jax.experimental import pallas as pl
from jax.experimental.pallas import tpu as pltpu

F32 = jnp.float32
BF16 = jnp.bfloat16

HEAD_DIM = 64
N_HEADS = 16
WIDTH = N_HEADS * HEAD_DIM
N_PAIRS = N_HEADS // 2
LANES = 128
LORA_COLS = 256
RWKV_COLS = 3 * WIDTH + LORA_COLS
NORM_EPS = 1e-6
GN_EPS = 64e-5
ATTN_SCALE = HEAD_DIM ** -0.5
EXP_NEG_HALF = 0.6065306597126334
NEG_BIG = -1e30

Z_COLS = 7680
COL_FQ, COL_FK, COL_FV, COL_OG = 0, 1024, 2048, 3072
COL_R, COL_K, COL_V, COL_L = 4096, 5120, 6144, 7168
COL_FL = 7424
N_Z_USED = 7440

VMEM_LIMIT = 56 * 1024 * 1024


def _cparams(sem):
    return pltpu.CompilerParams(dimension_semantics=sem, vmem_limit_bytes=VMEM_LIMIT)


def _sigmoid(x):
    return 1.0 / (1.0 + jnp.exp(-x))


def _dot(a, b):
    return jnp.dot(a, b, preferred_element_type=F32)


def _split2(x):
    hi = x.astype(BF16)
    lo = (x - hi.astype(F32)).astype(BF16)
    return hi, lo


def _split3(x):
    h1 = x.astype(BF16)
    r1 = x - h1.astype(F32)
    h2 = r1.astype(BF16)
    h3 = (r1 - h2.astype(F32)).astype(BF16)
    return h1, h2, h3


def _seg_sum(x, bd):
    parts = []
    for c in range(x.shape[1] // LANES):
        hi, lo = _split2(x[:, c * LANES:(c + 1) * LANES])
        parts.append(_dot(hi, bd) + _dot(lo, bd))
    return parts[0] if len(parts) == 1 else jnp.concatenate(parts, axis=1)


def _norm_matmul_kernel(x_ref, g_ref, w_ref, o_ref, xn_ref):
    @pl.when(pl.program_id(1) == 0)
    def _():
        x = x_ref[...]
        ms = jnp.mean(x * x, axis=-1, keepdims=True)
        xn_ref[...] = (x * lax.rsqrt(ms + NORM_EPS) * g_ref[...]).astype(BF16)

    o_ref[...] = _dot(xn_ref[...], w_ref[...])


def _norm_matmul(x, g, w, tm, tn):
    m, d = x.shape
    n = w.shape[1]
    return pl.pallas_call(
        _norm_matmul_kernel,
        out_shape=jax.ShapeDtypeStruct((m, n), F32),
        grid=(m // tm, n // tn),
        in_specs=[
            pl.BlockSpec((tm, d), lambda i, j: (i, 0)),
            pl.BlockSpec((1, d), lambda i, j: (0, 0)),
            pl.BlockSpec((d, tn), lambda i, j: (0, j)),
        ],
        out_specs=pl.BlockSpec((tm, tn), lambda i, j: (i, j)),
        scratch_shapes=[pltpu.VMEM((tm, d), BF16)],
        compiler_params=_cparams(("parallel", "arbitrary")),
        name="in_proj",
    )(x, g, w)


def _pre_math(zr, zk, zv, zl, pr, pk, pv, plo, mu_ref, w0_ref, a0_ref, kk_ref, ka_ref, rk_ref,
              wda_ref, wg_ref, bd_ref):
    bd = bd_ref[...]
    r = zr + (pr - zr) * mu_ref[:, 0:WIDTH]
    k = zk + (pk - zk) * mu_ref[:, WIDTH:2 * WIDTH]
    v = zv + (pv - zv) * mu_ref[:, 2 * WIDTH:3 * WIDTH]
    lo = zl + (plo - zl) * mu_ref[:, 3 * WIDTH:3 * WIDTH + LORA_COLS]
    l0 = lo[:, 0:LANES]
    lane = lax.broadcasted_iota(jnp.int32, l0.shape, 1)
    lhs = jnp.where(lane < HEAD_DIM, jnp.tanh(l0), l0).astype(BF16)
    da = _dot(lhs, wda_ref[...])
    decay = jnp.exp(-EXP_NEG_HALF * _sigmoid(w0_ref[...] + da[:, 0:WIDTH]))
    a = _sigmoid(a0_ref[...] + da[:, WIDTH:2 * WIDTH])
    g = _dot(_sigmoid(lo[:, LANES:2 * LANES]).astype(BF16), wg_ref[...])
    kk = k * kk_ref[...]
    nrm = jnp.sqrt(_seg_sum(kk * kk, bd))
    kkn = kk / jnp.maximum(nrm, 1e-12)
    k2 = k * (1.0 + (a - 1.0) * ka_ref[...])
    b = kkn * a
    bonus = _seg_sum(r * k2 * rk_ref[...], bd) * v
    return r, decay, k2, v, kkn, b, bonus, g


def _shift_rows(x, carry_ref):
    n = x.shape[0]
    prev0 = carry_ref[...]
    if n == 1:
        prev = prev0
    else:
        rolled = pltpu.roll(x, 1, 0)
        row = lax.broadcasted_iota(jnp.int32, x.shape, 0)
        prev = jnp.where(row == 0, prev0, rolled)
    carry_ref[...] = x[n - 1:n, :]
    return prev


def _pre_carry_kernel(zr_ref, zk_ref, zv_ref, zl_ref, fr_ref, fk_ref, fv_ref, fl_ref,
                      mu_ref, w0_ref, a0_ref, kk_ref, ka_ref, rk_ref, wda_ref, wg_ref, bd_ref,
                      r_o, w_o, k_o, v_o, kn_o, b_o, bo_o, g_o,
                      cr_ref, ck_ref, cv_ref, cl_ref):
    first = pl.program_id(1) == 0

    @pl.when(first)
    def _():
        cr_ref[...] = fr_ref[0]
        ck_ref[...] = fk_ref[0]
        cv_ref[...] = fv_ref[0]
        cl_ref[...] = fl_ref[0]

    zr, zk, zv, zl = zr_ref[0], zk_ref[0], zv_ref[0], zl_ref[0]
    pr = _shift_rows(zr, cr_ref)
    pk = _shift_rows(zk, ck_ref)
    pv = _shift_rows(zv, cv_ref)
    plo = _shift_rows(zl, cl_ref)
    outs = _pre_math(zr, zk, zv, zl, pr, pk, pv, plo, mu_ref, w0_ref, a0_ref, kk_ref, ka_ref,
                     rk_ref, wda_ref, wg_ref, bd_ref)
    for o_ref, val in zip((r_o, w_o, k_o, v_o, kn_o, b_o, bo_o, g_o), outs):
        o_ref[0] = val


def _pre_given_kernel(zr_ref, zk_ref, zv_ref, zl_ref, pr_ref, pk_ref, pv_ref, pl_ref,
                      mu_ref, w0_ref, a0_ref, kk_ref, ka_ref, rk_ref, wda_ref, wg_ref, bd_ref,
                      r_o, w_o, k_o, v_o, kn_o, b_o, bo_o, g_o):
    outs = _pre_math(zr_ref[0], zk_ref[0], zv_ref[0], zl_ref[0],
                     pr_ref[0], pk_ref[0], pv_ref[0], pl_ref[0],
                     mu_ref, w0_ref, a0_ref, kk_ref, ka_ref, rk_ref, wda_ref, wg_ref, bd_ref)
    for o_ref, val in zip((r_o, w_o, k_o, v_o, kn_o, b_o, bo_o, g_o), outs):
        o_ref[0] = val


def _rwkv_prelude(z3, prev3, params, tb, carry):
    bsz, t, _ = z3.shape
    nt = t // tb
    z_specs = [
        pl.BlockSpec((1, tb, WIDTH), lambda b, i: (b, i, COL_R // WIDTH)),
        pl.BlockSpec((1, tb, WIDTH), lambda b, i: (b, i, COL_K // WIDTH)),
        pl.BlockSpec((1, tb, WIDTH), lambda b, i: (b, i, COL_V // WIDTH)),
        pl.BlockSpec((1, tb, LORA_COLS), lambda b, i: (b, i, COL_L // LORA_COLS)),
    ]
    pt = 1 if carry else tb
    pidx = (lambda b, i: 0) if carry else (lambda b, i: i)
    p_specs = [
        pl.BlockSpec((1, pt, WIDTH), lambda b, i: (b, pidx(b, i), 0)),
        pl.BlockSpec((1, pt, WIDTH), lambda b, i: (b, pidx(b, i), 1)),
        pl.BlockSpec((1, pt, WIDTH), lambda b, i: (b, pidx(b, i), 2)),
        pl.BlockSpec((1, pt, LORA_COLS), lambda b, i: (b, pidx(b, i), 3 * WIDTH // LORA_COLS)),
    ]
    const = lambda shape: pl.BlockSpec(shape, lambda b, i: (0,) * len(shape))
    w_specs = [const((1, RWKV_COLS)), const((1, WIDTH)), const((1, WIDTH)), const((1, WIDTH)),
               const((1, WIDTH)), const((1, WIDTH)), const((LANES, 2 * WIDTH)),
               const((LANES, WIDTH)), const((LANES, LANES))]
    out_spec = pl.BlockSpec((1, tb, WIDTH), lambda b, i: (b, i, 0))
    n_out = 8
    out_shape = [jax.ShapeDtypeStruct((bsz, t, WIDTH), F32)] * n_out
    scratch = ([pltpu.VMEM((1, WIDTH), F32)] * 3 + [pltpu.VMEM((1, LORA_COLS), F32)]) if carry else []
    return pl.pallas_call(
        _pre_carry_kernel if carry else _pre_given_kernel,
        out_shape=out_shape,
        grid=(bsz, nt),
        in_specs=z_specs + p_specs + w_specs,
        out_specs=[out_spec] * n_out,
        scratch_shapes=scratch,
        compiler_params=_cparams(("parallel", "arbitrary")),
        name="rwkv_prelude",
    )(z3, z3, z3, z3, prev3, prev3, prev3, prev3, *params)


def _scan_kernel(r_ref, w_ref, k_ref, v_ref, kn_ref, b_ref, s0_ref, mask_ref, bd_ref,
                 bd2_ref, bdd_ref, o_ref, sout_ref, s_ref, vt_ref, *, bb, tt):
    ti = pl.program_id(1)

    @pl.when(ti == 0)
    def _():
        s_ref[...] = s0_ref[...]

    maskf = mask_ref[...]
    bd = bd_ref[...]
    bd2 = bd2_ref[...]
    bdd = bdd_ref[...]

    def tile(x):
        return jnp.concatenate(
            [jnp.broadcast_to(x[:, p * LANES:(p + 1) * LANES], (HEAD_DIM, LANES))
             for p in range(N_PAIRS)], axis=0)

    def v_diag(bi, t):
        return (maskf * tile(v_ref[bi, pl.ds(t, 1), :])).astype(BF16)

    for bi in range(bb):
        vt_ref[bi] = _dot(v_diag(bi, 0), bd)

    def step(t, carry):
        t_next = jnp.minimum(t + 1, tt - 1)
        row = lambda ref, bi: tile(ref[bi, pl.ds(t, 1), :])
        sa = []
        for bi in range(bb):
            hi, lo = _split2(s_ref[bi] * row(kn_ref, bi))
            sa.append(_dot(jnp.concatenate([hi, lo], axis=1), bd2))
        both = []
        for bi in range(bb):
            s_new = (s_ref[bi] * row(w_ref, bi) - sa[bi] * row(b_ref, bi)
                     + vt_ref[bi] * row(k_ref, bi))
            s_ref[bi] = s_new
            both.append(_dot(jnp.concatenate([(s_new * row(r_ref, bi)).astype(BF16),
                                              v_diag(bi, t_next)], axis=1), bdd))
        for bi in range(bb):
            vt_ref[bi] = both[bi][:, LANES:]
            of = maskf * both[bi][:, :LANES]
            o_ref[bi, pl.ds(t, 1), :] = jnp.concatenate(
                [jnp.sum(of[p * HEAD_DIM:(p + 1) * HEAD_DIM], axis=0, keepdims=True)
                 for p in range(N_PAIRS)], axis=1)
        return carry

    lax.fori_loop(0, tt, step, 0, unroll=min(tt, 4))

    @pl.when(ti == pl.num_programs(1) - 1)
    def _():
        sout_ref[...] = s_ref[...]


def _rwkv_scan(r, w, k, v, kn, b, s0, consts, tt):
    bsz, t, _ = r.shape
    bb = 2
    rows = N_PAIRS * HEAD_DIM
    seq_spec = pl.BlockSpec((bb, tt, WIDTH), lambda bi, ti: (bi, ti, 0))
    st_spec = pl.BlockSpec((bb, rows, LANES), lambda bi, ti: (bi, 0, 0))
    const = lambda shape: pl.BlockSpec(shape, lambda bi, ti: (0,) * len(shape))
    return pl.pallas_call(
        functools.partial(_scan_kernel, bb=bb, tt=tt),
        out_shape=[jax.ShapeDtypeStruct((bsz, t, WIDTH), F32),
                   jax.ShapeDtypeStruct((bsz, rows, LANES), F32)],
        grid=(bsz // bb, t // tt),
        in_specs=[seq_spec] * 6 + [st_spec, const((rows, LANES)), const((LANES, LANES)),
                                   const((2 * LANES, LANES)), const((2 * LANES, 2 * LANES))],
        out_specs=[seq_spec, st_spec],
        scratch_shapes=[pltpu.VMEM((bb, rows, LANES), F32), pltpu.VMEM((bb, rows, LANES), F32)],
        compiler_params=_cparams(("parallel", "arbitrary")),
        name="rwkv_scan",
    )(r, w, k, v, kn, b, s0, *consts)


def _pairs_from_heads(s):
    bsz = s.shape[0]
    s = s.reshape(bsz, N_PAIRS, 2, HEAD_DIM, HEAD_DIM).transpose(0, 1, 3, 2, 4)
    return s.reshape(bsz, N_PAIRS * HEAD_DIM, LANES)


def _heads_from_pairs(s):
    bsz = s.shape[0]
    s = s.reshape(bsz, N_PAIRS, HEAD_DIM, 2, HEAD_DIM).transpose(0, 1, 3, 2, 4)
    return s.reshape(bsz, N_HEADS, HEAD_DIM, HEAD_DIM)


def _post_kernel(o_ref, bo_ref, g_ref, lg_ref, lb_ref, bd_ref, out_ref):
    bd = bd_ref[...]
    o = o_ref[0]
    mean = _seg_sum(o, bd) * (1.0 / HEAD_DIM)
    d = o - mean
    var = _seg_sum(d * d, bd) * (1.0 / HEAD_DIM)
    on = d * lax.rsqrt(var + GN_EPS) * lg_ref[...] + lb_ref[...]
    out_ref[0] = ((on + bo_ref[0]) * g_ref[0]).astype(BF16)


def _rwkv_post(o, bonus, g, lnx_g, lnx_b, bd, tb):
    bsz, t, _ = o.shape
    spec = pl.BlockSpec((1, tb, WIDTH), lambda b, i: (b, i, 0))
    const = lambda shape: pl.BlockSpec(shape, lambda b, i: (0,) * len(shape))
    return pl.pallas_call(
        _post_kernel,
        out_shape=jax.ShapeDtypeStruct((bsz, t, WIDTH), BF16),
        grid=(bsz, t // tb),
        in_specs=[spec, spec, spec, const((1, WIDTH)), const((1, WIDTH)), const((LANES, LANES))],
        out_specs=spec,
        compiler_params=_cparams(("parallel", "parallel")),
        name="rwkv_post",
    )(o, bonus, g, lnx_g, lnx_b, bd)


def _fox_norm(x, gain, bd):
    ms = _seg_sum(x * x, bd) * (1.0 / HEAD_DIM)
    return x * lax.rsqrt(ms + NORM_EPS) * gain


def _log_sigmoid(x):
    return jnp.minimum(x, 0.0) - jnp.log(1.0 + jnp.exp(-jnp.abs(x)))


def _fox_prep_kernel(fq_ref, fk_ref, fv_ref, fl_ref, bf_ref, qg_ref, kg_ref, bd_ref, tril_ref,
                     pq_ref, pk_ref, kn_o, v_o, lf_o, qa_o, ka_o, va_o, c_ref):
    bd = bd_ref[...]
    q = _fox_norm(fq_ref[0], qg_ref[...], bd) * ATTN_SCALE
    kn = _fox_norm(fk_ref[0], kg_ref[...], bd)
    v = fv_ref[0]
    kn_o[0] = kn
    v_o[0] = v
    lane = lax.broadcasted_iota(jnp.int32, fl_ref[0].shape, 1)
    lf = jnp.where(lane < N_HEADS, _log_sigmoid(fl_ref[0] + bf_ref[...]), 0.0)
    lf_o[0] = lf

    @pl.when(pl.program_id(1) == 0)
    def _():
        c_ref[...] = jnp.zeros_like(c_ref)

    tril = tril_ref[...]
    h1, h2, h3 = _split3(lf)
    c = _dot(tril, h1) + _dot(tril, h2) + _dot(tril, h3) + c_ref[...]
    n = c.shape[0]
    c_ref[...] = c[n - 1:n, :]
    c1, c2, c3 = _split3(c)
    cpack = (c1.astype(F32) + pltpu.roll(c2.astype(F32), N_HEADS, 1)
             + pltpu.roll(c3.astype(F32), 2 * N_HEADS, 1)).astype(BF16)
    lane_v = lax.broadcasted_iota(jnp.int32, (n, LANES), 1)
    ones_q = jnp.where((lane_v >= HEAD_DIM + 3) & (lane_v < HEAD_DIM + 6), 1.0, 0.0)
    ones_k = jnp.where((lane_v >= HEAD_DIM) & (lane_v < HEAD_DIM + 3), 1.0, 0.0)
    for p in range(N_PAIRS):
        sl = slice(p * LANES, (p + 1) * LANES)
        lhs_q = jnp.concatenate([q[:, sl].astype(BF16), cpack], axis=1)
        lhs_k = jnp.concatenate([kn[:, sl].astype(BF16), cpack], axis=1)
        vs = v[:, sl]
        for hh in range(2):
            h = 2 * p + hh
            qa_o[0, h] = (_dot(lhs_q, pq_ref[h]) + ones_q).astype(BF16)
            ka_o[0, h] = (_dot(lhs_k, pk_ref[h]) + ones_k).astype(BF16)
            if hh == 0:
                va = jnp.where(lane_v < HEAD_DIM, vs, jnp.where(lane_v == HEAD_DIM, 1.0, 0.0))
            else:
                va = jnp.where(lane_v >= HEAD_DIM, vs, jnp.where(lane_v == 0, 1.0, 0.0))
            va_o[0, h] = va.astype(BF16)


def _fox_prep_plain_kernel(fq_ref, fk_ref, fv_ref, fl_ref, bf_ref, qg_ref, kg_ref, bd_ref,
                           kn_o, v_o, lf_o, q_o):
    bd = bd_ref[...]
    q_o[0] = _fox_norm(fq_ref[0], qg_ref[...], bd) * ATTN_SCALE
    kn_o[0] = _fox_norm(fk_ref[0], kg_ref[...], bd)
    v_o[0] = fv_ref[0]
    lane = lax.broadcasted_iota(jnp.int32, fl_ref[0].shape, 1)
    lf_o[0] = jnp.where(lane < N_HEADS, _log_sigmoid(fl_ref[0] + bf_ref[...]), 0.0)


def _fox_prep(z3, params, tb, augment):
    bsz, t, _ = z3.shape
    z_specs = [
        pl.BlockSpec((1, tb, WIDTH), lambda b, i: (b, i, COL_FQ // WIDTH)),
        pl.BlockSpec((1, tb, WIDTH), lambda b, i: (b, i, COL_FK // WIDTH)),
        pl.BlockSpec((1, tb, WIDTH), lambda b, i: (b, i, COL_FV // WIDTH)),
        pl.BlockSpec((1, tb, LANES), lambda b, i: (b, i, COL_FL // LANES)),
    ]
    const = lambda shape: pl.BlockSpec(shape, lambda b, i: (0,) * len(shape))
    w_specs = [const((1, LANES)), const((1, WIDTH)), const((1, WIDTH)), const((LANES, LANES))]
    row_spec = pl.BlockSpec((1, tb, WIDTH), lambda b, i: (b, i, 0))
    lf_spec = pl.BlockSpec((1, tb, LANES), lambda b, i: (b, i, 0))
    row_shape = jax.ShapeDtypeStruct((bsz, t, WIDTH), F32)
    lf_shape = jax.ShapeDtypeStruct((bsz, t, LANES), F32)
    if not augment:
        return pl.pallas_call(
            _fox_prep_plain_kernel,
            out_shape=[row_shape, row_shape, lf_shape, row_shape],
            grid=(bsz, t // tb),
            in_specs=z_specs + w_specs,
            out_specs=[row_spec, row_spec, lf_spec, row_spec],
            compiler_params=_cparams(("parallel", "arbitrary")),
            name="fox_prep_sample",
        )(z3, z3, z3, z3, *params[:4])
    aug_spec = pl.BlockSpec((1, N_HEADS, tb, LANES), lambda b, i: (b, 0, i, 0))
    aug_shape = jax.ShapeDtypeStruct((bsz, N_HEADS, t, LANES), BF16)
    w_specs += [const((tb, tb)), const((N_HEADS, 2 * LANES, LANES)),
                const((N_HEADS, 2 * LANES, LANES))]
    return pl.pallas_call(
        _fox_prep_kernel,
        out_shape=[row_shape, row_shape, lf_shape, aug_shape, aug_shape, aug_shape],
        grid=(bsz, t // tb),
        in_specs=z_specs + w_specs,
        out_specs=[row_spec, row_spec, lf_spec, aug_spec, aug_spec, aug_spec],
        scratch_shapes=[pltpu.VMEM((1, LANES), F32)],
        compiler_params=_cparams(("parallel", "arbitrary")),
        name="fox_prep",
    )(z3, z3, z3, z3, *params)


def _place_matrices():
    h = jnp.arange(N_HEADS)[:, None, None]
    row = jnp.arange(2 * LANES)[None, :, None]
    col = jnp.arange(LANES)[None, None, :]
    data = (row < LANES) & (row == (h % 2) * HEAD_DIM + col) & (col < HEAD_DIM)
    piece = (row - LANES) // N_HEADS
    is_c = (row >= LANES) & (row < LANES + 3 * N_HEADS) & ((row - LANES) % N_HEADS == h)
    cq = is_c & (col == HEAD_DIM + piece)
    ck = is_c & (col == HEAD_DIM + 3 + piece)
    pq = (data | cq).astype(F32)
    pk = data.astype(F32) - ck.astype(F32)
    return pq.astype(BF16), pk.astype(BF16)


def _flash_kernel(q_ref, k_ref, v_ref, og_ref, o_ref, *, tq, tk, nh):
    qi = pl.program_id(2)
    row = lax.broadcasted_iota(jnp.int32, (tq, tk), 0)
    col = lax.broadcasted_iota(jnp.int32, (tq, tk), 1)
    causal = col <= row
    qs = [q_ref[0, hh] for hh in range(nh)]

    def block(j, carry, masked):
        start = pl.multiple_of(j * tk, tk)
        new = []
        for hh in range(nh):
            m, acc = carry[2 * hh], carry[2 * hh + 1]
            kb = k_ref[0, hh, pl.ds(start, tk), :]
            vb = v_ref[0, hh, pl.ds(start, tk), :]
            s = lax.dot_general(qs[hh], kb, (((1,), (1,)), ((), ())), preferred_element_type=F32)
            if masked:
                s = jnp.where(causal, s, NEG_BIG)
            m_new = jnp.maximum(m, jnp.max(s, axis=1, keepdims=True))
            p = jnp.exp(s - m_new)
            new += [m_new, jnp.exp(m - m_new) * acc + _dot(p.astype(BF16), vb)]
        return tuple(new)

    init = (jnp.full((tq, 1), NEG_BIG, F32), jnp.zeros((tq, LANES), F32)) * nh
    carry = lax.fori_loop(0, qi, lambda j, c: block(j, c, False), init)
    carry = block(qi, carry, True)
    lane = lax.broadcasted_iota(jnp.int32, (tq, LANES), 1)
    for pp in range(nh // 2):
        acc_e, acc_o = carry[4 * pp + 1], carry[4 * pp + 3]
        o = jnp.where(lane < HEAD_DIM, acc_e / acc_e[:, HEAD_DIM:HEAD_DIM + 1],
                      acc_o / acc_o[:, 0:1])
        sl = slice(pp * LANES, (pp + 1) * LANES)
        o_ref[0, :, sl] = (o * _sigmoid(og_ref[0, :, sl])).astype(BF16)


def _fox_flash(qa, ka, va, z3, tq, nh):
    bsz, _, t, _ = qa.shape
    tk = tq
    gw = (nh // 2) * LANES
    return pl.pallas_call(
        functools.partial(_flash_kernel, tq=tq, tk=tk, nh=nh),
        out_shape=jax.ShapeDtypeStruct((bsz, t, WIDTH), BF16),
        grid=(bsz, N_HEADS // nh, t // tq),
        in_specs=[
            pl.BlockSpec((1, nh, tq, LANES), lambda b, g, i: (b, g, i, 0)),
            pl.BlockSpec((1, nh, t, LANES), lambda b, g, i: (b, g, 0, 0)),
            pl.BlockSpec((1, nh, t, LANES), lambda b, g, i: (b, g, 0, 0)),
            pl.BlockSpec((1, tq, gw), lambda b, g, i: (b, i, COL_OG // gw + g)),
        ],
        out_specs=pl.BlockSpec((1, tq, gw), lambda b, g, i: (b, i, g)),
        compiler_params=_cparams(("parallel", "parallel", "arbitrary")),
        name="fox_flash",
    )(qa, ka, va, z3)


def _decode_kernel(pt_ref, qt_ref, kc_ref, vc_ref, lc_ref, og_ref, lt_ref, on_ref, *rest, npg):
    k_refs = rest[0:npg]
    v_refs = rest[npg:2 * npg]
    l_refs = rest[2 * npg:3 * npg]
    o_ref = rest[3 * npg]
    m_ref, l_ref, cy_ref, acc_ref = rest[3 * npg + 1:]
    step = pl.program_id(1)
    later = lt_ref[...]
    ones = on_ref[...]

    def rows(fn):
        return jnp.concatenate([fn(h) for h in range(N_HEADS)], axis=0)

    @pl.when(step == 0)
    def _():
        m_ref[...] = rows(lambda h: jnp.sum(qt_ref[0, h] * kc_ref[0, h], axis=0, keepdims=True))
        l_ref[...] = jnp.ones_like(l_ref)
        cy_ref[...] = lc_ref[0]
        lane = lax.broadcasted_iota(jnp.int32, (HEAD_DIM, LANES), 1)
        for h in range(N_HEADS):
            acc_ref[h] = jnp.where(lane == 0, vc_ref[0, h], 0.0)

    for i in range(npg):
        kp, vp = k_refs[i], v_refs[i]
        lf = l_refs[i][0]
        h1, h2, h3 = _split3(lf)
        bias = _dot(h1, later) + _dot(h2, later) + _dot(h3, later) + cy_ref[...]
        cy_ref[...] = cy_ref[...] + _dot(h1, ones) + _dot(h2, ones) + _dot(h3, ones)
        s = rows(lambda h: jnp.sum(qt_ref[0, h] * kp[0, h], axis=0, keepdims=True)) + bias
        m_old = m_ref[...]
        m_new = jnp.maximum(m_old, jnp.max(s, axis=1, keepdims=True))
        alpha = jnp.exp(m_old - m_new)
        p = jnp.exp(s - m_new)
        m_ref[...] = m_new
        l_ref[...] = alpha * l_ref[...] + jnp.sum(p, axis=1, keepdims=True)
        for h in range(N_HEADS):
            acc_ref[h] = alpha[h:h + 1, :] * acc_ref[h] + p[h:h + 1, :] * vp[0, h]

    @pl.when(step == pl.num_programs(1) - 1)
    def _():
        ones8 = ones[0:8, :]

        def token_sum(h):
            nt = lambda a: lax.dot_general(ones8, a, (((1,), (1,)), ((), ())),
                                           preferred_element_type=F32)
            a1, a2, a3 = _split3(acc_ref[h])
            return (nt(a1) + nt(a2) + nt(a3))[0:1, :]

        o_ref[0] = (rows(token_sum) / l_ref[:, 0:HEAD_DIM]) * _sigmoid(og_ref[0])


def _fox_decode(page_table, qt, kc, vc, lfc, og, cache_kt, cache_vt, cache_lt, later, ones, npg):
    dbsz, n_pages = page_table.shape
    page = cache_kt.shape[-1]
    steps = n_pages // npg

    def page_map(i, nd):
        return lambda b, s, pt: (pt[b, n_pages - 1 - (s * npg + i)],) + (0,) * nd

    tile_spec = pl.BlockSpec((1, N_HEADS, HEAD_DIM, page), lambda b, s, pt: (b, 0, 0, 0))
    in_specs = [
        tile_spec, tile_spec, tile_spec,
        pl.BlockSpec((1, N_HEADS, page), lambda b, s, pt: (b, 0, 0)),
        pl.BlockSpec((1, N_HEADS, HEAD_DIM), lambda b, s, pt: (b, 0, 0)),
        pl.BlockSpec((page, page), lambda b, s, pt: (0, 0)),
        pl.BlockSpec((page, page), lambda b, s, pt: (0, 0)),
    ]
    in_specs += [pl.BlockSpec((1, N_HEADS, HEAD_DIM, page), page_map(i, 3)) for i in range(npg)]
    in_specs += [pl.BlockSpec((1, N_HEADS, HEAD_DIM, page), page_map(i, 3)) for i in range(npg)]
    in_specs += [pl.BlockSpec((1, N_HEADS, page), page_map(i, 2)) for i in range(npg)]
    stat = pltpu.VMEM((N_HEADS, page), F32)
    grid_spec = pltpu.PrefetchScalarGridSpec(
        num_scalar_prefetch=1,
        grid=(dbsz, steps),
        in_specs=in_specs,
        out_specs=pl.BlockSpec((1, N_HEADS, HEAD_DIM), lambda b, s, pt: (b, 0, 0)),
        scratch_shapes=[stat, stat, stat, pltpu.VMEM((N_HEADS, HEAD_DIM, page), F32)],
    )
    return pl.pallas_call(
        functools.partial(_decode_kernel, npg=npg),
        out_shape=jax.ShapeDtypeStruct((dbsz, N_HEADS, HEAD_DIM), F32),
        grid_spec=grid_spec,
        compiler_params=_cparams(("parallel", "arbitrary")),
        name="fox_decode",
    )(page_table, qt, kc, vc, lfc, og, later, ones,
      *([cache_kt] * npg), *([cache_vt] * npg), *([cache_lt] * npg))


def _out_proj_kernel(x_ref, oa_ref, ob_ref, wa_ref, wb_ref, h_ref):
    h_ref[...] = x_ref[...] + _dot(oa_ref[...], wa_ref[...]) + _dot(ob_ref[...], wb_ref[...])


def _out_proj(x, oa, ob, w_out, tm, tn):
    m, d = x.shape
    return pl.pallas_call(
        _out_proj_kernel,
        out_shape=jax.ShapeDtypeStruct((m, d), F32),
        grid=(m // tm, d // tn),
        in_specs=[
            pl.BlockSpec((tm, tn), lambda i, j: (i, j)),
            pl.BlockSpec((tm, WIDTH), lambda i, j: (i, 0)),
            pl.BlockSpec((tm, WIDTH), lambda i, j: (i, 0)),
            pl.BlockSpec((WIDTH, tn), lambda i, j: (0, j)),
            pl.BlockSpec((WIDTH, tn), lambda i, j: (1, j)),
        ],
        out_specs=pl.BlockSpec((tm, tn), lambda i, j: (i, j)),
        compiler_params=_cparams(("parallel", "arbitrary")),
        name="out_proj",
    )(x, oa, ob, w_out, w_out)


def _ffn_kernel(h_ref, gn_ref, wg_ref, wu_ref, wd_ref, gf_ref, y_ref, hn_ref, acc_ref):
    f = pl.program_id(1)

    @pl.when(f == 0)
    def _():
        h = h_ref[...]
        ms = jnp.mean(h * h, axis=-1, keepdims=True)
        hn_ref[...] = (h * lax.rsqrt(ms + NORM_EPS) * gn_ref[...]).astype(BF16)
        acc_ref[...] = jnp.zeros_like(acc_ref)

    hn = hn_ref[...]
    gate = _dot(hn, wg_ref[...])
    up = _dot(hn, wu_ref[...])
    act = (gate * _sigmoid(gate) * up).astype(BF16)
    acc_ref[...] += _dot(act, wd_ref[...])

    @pl.when(f == pl.num_programs(1) - 1)
    def _():
        hh = h_ref[...] + acc_ref[...]
        ms = jnp.mean(hh * hh, axis=-1, keepdims=True)
        y_ref[...] = hh * lax.rsqrt(ms + NORM_EPS) * gf_ref[...]


def _ffn(h, g_ffn, wg, wu, wd, g_final, tm, tf):
    m, d = h.shape
    dff = wg.shape[1]
    return pl.pallas_call(
        _ffn_kernel,
        out_shape=jax.ShapeDtypeStruct((m, d), F32),
        grid=(m // tm, dff // tf),
        in_specs=[
            pl.BlockSpec((tm, d), lambda i, f: (i, 0)),
            pl.BlockSpec((1, d), lambda i, f: (0, 0)),
            pl.BlockSpec((d, tf), lambda i, f: (0, f)),
            pl.BlockSpec((d, tf), lambda i, f: (0, f)),
            pl.BlockSpec((tf, d), lambda i, f: (f, 0)),
            pl.BlockSpec((1, d), lambda i, f: (0, 0)),
        ],
        out_specs=pl.BlockSpec((tm, d), lambda i, f: (i, 0)),
        scratch_shapes=[pltpu.VMEM((tm, d), BF16), pltpu.VMEM((tm, d), F32)],
        compiler_params=_cparams(("parallel", "arbitrary")),
        name="ffn",
    )(h, g_ffn, wg, wu, wd, g_final)


def _block_diag_ones():
    i = jnp.arange(LANES)
    return (i[:, None] // HEAD_DIM == i[None, :] // HEAD_DIM).astype(BF16)


def _pick(n, cap):
    t = min(n, cap)
    while n % t:
        t //= 2
    return t


def kernel(x_prompt, x_sample, cache_k, cache_v, cache_logf, state_wkv, state_shift, page_table,
           norm_mix, w_in, mu_shift, w0, w_decay_up, a0, w_aaa_up, w_gate_up, k_k, k_a, r_k,
           lnx_g, lnx_b, b_f, q_norm, k_norm, w_out, norm_ffn, w_ffn_gate, w_ffn_up, w_ffn_down,
           norm_final):
    bsz, seq, d = x_prompt.shape
    dbsz = x_sample.shape[0]
    lyr = 0
    rw = RWKV_COLS

    wi = w_in[lyr]
    w_z = jnp.concatenate(
        [wi[:, rw:rw + 4 * WIDTH], wi[:, :rw], wi[:, rw + 4 * WIDTH:],
         jnp.zeros((d, Z_COLS - N_Z_USED), F32)], axis=1).astype(BF16)
    g_mix = norm_mix[lyr][None, :]
    wda = jnp.zeros((LANES, 2 * WIDTH), F32)
    wda = wda.at[:HEAD_DIM, :WIDTH].set(w_decay_up[lyr]).at[HEAD_DIM:, WIDTH:].set(w_aaa_up[lyr])
    bd = _block_diag_ones()
    bd2 = jnp.concatenate([bd, bd], axis=0)
    pre_params = (mu_shift[lyr][None, :], w0[lyr][None, :], a0[lyr][None, :], k_k[lyr][None, :],
                  k_a[lyr][None, :], r_k[lyr].reshape(1, WIDTH), wda.astype(BF16),
                  w_gate_up[lyr].astype(BF16), bd)
    li = jnp.arange(LANES)
    maskf = (li[None, :] % HEAD_DIM
             == jnp.arange(N_PAIRS * HEAD_DIM)[:, None] % HEAD_DIM).astype(F32)
    zb = jnp.zeros_like(bd)
    bdd = jnp.concatenate([jnp.concatenate([bd, zb], axis=1),
                           jnp.concatenate([zb, bd], axis=1)], axis=0)
    scan_consts = (maskf, bd, bd2, bdd)
    lg, lb = lnx_g[lyr][None, :], lnx_b[lyr][None, :]
    bfp = jnp.zeros((1, LANES), F32).at[0, :N_HEADS].set(b_f[lyr])
    qg = jnp.tile(q_norm[lyr], N_HEADS)[None, :]
    kg = jnp.tile(k_norm[lyr], N_HEADS)[None, :]
    tb_fox = _pick(seq, 256)
    tril = (jnp.arange(tb_fox)[:, None] >= jnp.arange(tb_fox)[None, :]).astype(BF16)
    pq, pk = _place_matrices()
    w_o = w_out[lyr].astype(BF16)
    wg = w_ffn_gate[lyr].astype(BF16)
    wu = w_ffn_up[lyr].astype(BF16)
    wd = w_ffn_down[lyr].astype(BF16)
    g_ffn = norm_ffn[lyr][None, :]
    g_fin = norm_final[None, :]

    m_p = bsz * seq
    xp = x_prompt.reshape(m_p, d)
    z_p = _norm_matmul(xp, g_mix, w_z, _pick(m_p, 1024), 512)
    z3p = z_p.reshape(bsz, seq, Z_COLS)
    first = jnp.zeros((bsz, 1, rw), F32)
    r, wdec, k2, v, kn, bb_, bonus, gate = _rwkv_prelude(z3p, first, pre_params,
                                                        _pick(seq, 256), True)
    s0 = jnp.zeros((bsz, N_PAIRS * HEAD_DIM, LANES), F32)
    o_p, s_p = _rwkv_scan(r, wdec, k2, v, kn, bb_, s0, scan_consts, _pick(seq, 128))
    oa_p = _rwkv_post(o_p, bonus, gate, lg, lb, bd, _pick(seq, 512))
    fkn_p, fv_p, lf_p, qa, ka, va = _fox_prep(z3p, (bfp, qg, kg, bd, tril, pq, pk), tb_fox, True)
    ob_p = _fox_flash(qa, ka, va, z3p, _pick(seq, 256), 4)
    h_p = _out_proj(xp, oa_p.reshape(m_p, WIDTH), ob_p.reshape(m_p, WIDTH), w_o,
                    _pick(m_p, 1024), 512)
    y_p = _ffn(h_p, g_ffn, wg, wu, wd, g_fin, _pick(m_p, 512), 512)

    xs = x_sample.reshape(dbsz, d)
    z_s = _norm_matmul(xs, g_mix, w_z, dbsz, 512)
    z3s = z_s.reshape(1, dbsz, Z_COLS)
    prev_s = state_shift[lyr].reshape(1, dbsz, rw)
    outs = _rwkv_prelude(z3s, prev_s, pre_params, dbsz, False)
    r, wdec, k2, v, kn, bb_, bonus, gate = [a.reshape(dbsz, 1, WIDTH) for a in outs]
    o_s, s_s = _rwkv_scan(r, wdec, k2, v, kn, bb_, _pairs_from_heads(state_wkv[lyr]),
                          scan_consts, 1)
    oa_s = _rwkv_post(o_s.reshape(1, dbsz, WIDTH), bonus.reshape(1, dbsz, WIDTH),
                      gate.reshape(1, dbsz, WIDTH), lg, lb, bd, dbsz)
    fkn_s, fv_s, lf_s, q_s = _fox_prep(z3s, (bfp, qg, kg, bd), dbsz, False)
    page = cache_k.shape[2]
    lane_tile = lambda a: jnp.broadcast_to(a.reshape(dbsz, N_HEADS, HEAD_DIM, 1),
                                           (dbsz, N_HEADS, HEAD_DIM, page))
    lfc = jnp.broadcast_to(lf_s[0, :, :N_HEADS, None], (dbsz, N_HEADS, page))
    og_s = z_s[:, COL_OG:COL_OG + WIDTH].reshape(dbsz, N_HEADS, HEAD_DIM)
    later = (jnp.arange(page)[:, None] > jnp.arange(page)[None, :]).astype(BF16)
    ob_s = _fox_decode(page_table, lane_tile(q_s), lane_tile(fkn_s), lane_tile(fv_s), lfc, og_s,
                       cache_k[lyr].transpose(0, 2, 3, 1), cache_v[lyr].transpose(0, 2, 3, 1),
                       cache_logf[lyr].transpose(0, 2, 1), later, jnp.ones((page, page), BF16),
                       _pick(page_table.shape[1], 8))
    h_s = _out_proj(xs, oa_s.reshape(dbsz, WIDTH), ob_s.reshape(dbsz, WIDTH).astype(BF16),
                    w_o, dbsz, 512)
    y_s = _ffn(h_s, g_ffn, wg, wu, wd, g_fin, dbsz, 512)

    hd = (N_HEADS, HEAD_DIM)
    return (
        y_p.reshape(bsz, seq, d),
        y_s.reshape(dbsz, 1, d),
        fkn_p.reshape(1, bsz, seq, *hd),
        fv_p.reshape(1, bsz, seq, *hd),
        lf_p[:, :, :N_HEADS].reshape(1, bsz, seq, N_HEADS),
        _heads_from_pairs(s_p)[None],
        z3p[:, seq - 1, COL_R:COL_R + rw][None],
        fkn_s.reshape(1, dbsz, 1, *hd),
        fv_s.reshape(1, dbsz, 1, *hd),
        lf_s[0, :, :N_HEADS].reshape(1, dbsz, 1, N_HEADS),
        _heads_from_pairs(s_s)[None],
        z_s[:, COL_R:COL_R + rw][None],
    )
```

```python
import functools

import jax
import jax.numpy as jnp
from jax import lax
from jax.experimental import pallas as pl
from jax.experimental.pallas import tpu as pltpu

F32 = jnp.float32
BF16 = jnp.bfloat16

HEAD_DIM = 64
N_HEADS = 16
WIDTH = N_HEADS * HEAD_DIM
N_PAIRS = N_HEADS // 2
LANES = 128
LORA_COLS = 256
RWKV_COLS = 3 * WIDTH + LORA_COLS
NORM_EPS = 1e-6
GN_EPS = 64e-5
ATTN_SCALE = HEAD_DIM ** -0.5
EXP_NEG_HALF = 0.6065306597126334
NEG_BIG = -1e30

Z_COLS = 7680
COL_FQ, COL_FK, COL_FV, COL_OG = 0, 1024, 2048, 3072
COL_R, COL_K, COL_V, COL_L = 4096, 5120, 6144, 7168
COL_FL = 7424
N_Z_USED = 7440

VMEM_LIMIT = 56 * 1024 * 1024


def _cparams(sem):
    return pltpu.CompilerParams(dimension_semantics=sem, vmem_limit_bytes=VMEM_LIMIT)


def _sigmoid(x):
    return 1.0 / (1.0 + jnp.exp(-x))


def _dot(a, b):
    return jnp.dot(a, b, preferred_element_type=F32)


def _split2(x):
    hi = x.astype(BF16)
    lo = (x - hi.astype(F32)).astype(BF16)
    return hi, lo


def _split3(x):
    h1 = x.astype(BF16)
    r1 = x - h1.astype(F32)
    h2 = r1.astype(BF16)
    h3 = (r1 - h2.astype(F32)).astype(BF16)
    return h1, h2, h3


def _seg_sum(x, bd):
    parts = []
    for c in range(x.shape[1] // LANES):
        hi, lo = _split2(x[:, c * LANES:(c + 1) * LANES])
        parts.append(_dot(hi, bd) + _dot(lo, bd))
    return parts[0] if len(parts) == 1 else jnp.concatenate(parts, axis=1)


def _norm_matmul_kernel(x_ref, g_ref, w_ref, o_ref, xn_ref):
    @pl.when(pl.program_id(1) == 0)
    def _():
        x = x_ref[...]
        ms = jnp.mean(x * x, axis=-1, keepdims=True)
        xn_ref[...] = (x * lax.rsqrt(ms + NORM_EPS) * g_ref[...]).astype(BF16)

    o_ref[...] = _dot(xn_ref[...], w_ref[...])


def _norm_matmul(x, g, w, tm, tn):
    m, d = x.shape
    n = w.shape[1]
    return pl.pallas_call(
        _norm_matmul_kernel,
        out_shape=jax.ShapeDtypeStruct((m, n), F32),
        grid=(m // tm, n // tn),
        in_specs=[
            pl.BlockSpec((tm, d), lambda i, j: (i, 0)),
            pl.BlockSpec((1, d), lambda i, j: (0, 0)),
            pl.BlockSpec((d, tn), lambda i, j: (0, j)),
        ],
        out_specs=pl.BlockSpec((tm, tn), lambda i, j: (i, j)),
        scratch_shapes=[pltpu.VMEM((tm, d), BF16)],
        compiler_params=_cparams(("parallel", "arbitrary")),
        name="in_proj",
    )(x, g, w)


def _pre_math(zr, zk, zv, zl, pr, pk, pv, plo, mu_ref, w0_ref, a0_ref, kk_ref, ka_ref, rk_ref,
              wda_ref, wg_ref, bd_ref):
    bd = bd_ref[...]
    r = zr + (pr - zr) * mu_ref[:, 0:WIDTH]
    k = zk + (pk - zk) * mu_ref[:, WIDTH:2 * WIDTH]
    v = zv + (pv - zv) * mu_ref[:, 2 * WIDTH:3 * WIDTH]
    lo = zl + (plo - zl) * mu_ref[:, 3 * WIDTH:3 * WIDTH + LORA_COLS]
    l0 = lo[:, 0:LANES]
    lane = lax.broadcasted_iota(jnp.int32, l0.shape, 1)
    lhs = jnp.where(lane < HEAD_DIM, jnp.tanh(l0), l0).astype(BF16)
    da = _dot(lhs, wda_ref[...])
    decay = jnp.exp(-EXP_NEG_HALF * _sigmoid(w0_ref[...] + da[:, 0:WIDTH]))
    a = _sigmoid(a0_ref[...] + da[:, WIDTH:2 * WIDTH])
    g = _dot(_sigmoid(lo[:, LANES:2 * LANES]).astype(BF16), wg_ref[...])
    kk = k * kk_ref[...]
    nrm = jnp.sqrt(_seg_sum(kk * kk, bd))
    kkn = kk / jnp.maximum(nrm, 1e-12)
    k2 = k * (1.0 + (a - 1.0) * ka_ref[...])
    b = kkn * a
    bonus = _seg_sum(r * k2 * rk_ref[...], bd) * v
    return r, decay, k2, v, kkn, b, bonus, g


def _shift_rows(x, carry_ref):
    n = x.shape[0]
    prev0 = carry_ref[...]
    if n == 1:
        prev = prev0
    else:
        rolled = pltpu.roll(x, 1, 0)
        row = lax.broadcasted_iota(jnp.int32, x.shape, 0)
        prev = jnp.where(row == 0, prev0, rolled)
    carry_ref[...] = x[n - 1:n, :]
    return prev


N_SCAN_SEQS = 6


def _store_pre_outputs(o_refs, vals):
    for idx, (o_ref, val) in enumerate(zip(o_refs, vals)):
        if idx < N_SCAN_SEQS:
            for p in range(N_PAIRS):
                o_ref[0, p] = val[:, p * LANES:(p + 1) * LANES]
        else:
            o_ref[0] = val


def _pre_carry_kernel(zr_ref, zk_ref, zv_ref, zl_ref, fr_ref, fk_ref, fv_ref, fl_ref,
                      mu_ref, w0_ref, a0_ref, kk_ref, ka_ref, rk_ref, wda_ref, wg_ref, bd_ref,
                      r_o, w_o, k_o, v_o, kn_o, b_o, bo_o, g_o,
                      cr_ref, ck_ref, cv_ref, cl_ref):
    first = pl.program_id(1) == 0

    @pl.when(first)
    def _():
        cr_ref[...] = fr_ref[0]
        ck_ref[...] = fk_ref[0]
        cv_ref[...] = fv_ref[0]
        cl_ref[...] = fl_ref[0]

    zr, zk, zv, zl = zr_ref[0], zk_ref[0], zv_ref[0], zl_ref[0]
    pr = _shift_rows(zr, cr_ref)
    pk = _shift_rows(zk, ck_ref)
    pv = _shift_rows(zv, cv_ref)
    plo = _shift_rows(zl, cl_ref)
    outs = _pre_math(zr, zk, zv, zl, pr, pk, pv, plo, mu_ref, w0_ref, a0_ref, kk_ref, ka_ref,
                     rk_ref, wda_ref, wg_ref, bd_ref)
    _store_pre_outputs((r_o, w_o, k_o, v_o, kn_o, b_o, bo_o, g_o), outs)


def _pre_given_kernel(zr_ref, zk_ref, zv_ref, zl_ref, pr_ref, pk_ref, pv_ref, pl_ref,
                      mu_ref, w0_ref, a0_ref, kk_ref, ka_ref, rk_ref, wda_ref, wg_ref, bd_ref,
                      r_o, w_o, k_o, v_o, kn_o, b_o, bo_o, g_o):
    outs = _pre_math(zr_ref[0], zk_ref[0], zv_ref[0], zl_ref[0],
                     pr_ref[0], pk_ref[0], pv_ref[0], pl_ref[0],
                     mu_ref, w0_ref, a0_ref, kk_ref, ka_ref, rk_ref, wda_ref, wg_ref, bd_ref)
    _store_pre_outputs((r_o, w_o, k_o, v_o, kn_o, b_o, bo_o, g_o), outs)


def _rwkv_prelude(z3, prev3, params, tb, carry):
    bsz, t, _ = z3.shape
    nt = t // tb
    z_specs = [
        pl.BlockSpec((1, tb, WIDTH), lambda b, i: (b, i, COL_R // WIDTH)),
        pl.BlockSpec((1, tb, WIDTH), lambda b, i: (b, i, COL_K // WIDTH)),
        pl.BlockSpec((1, tb, WIDTH), lambda b, i: (b, i, COL_V // WIDTH)),
        pl.BlockSpec((1, tb, LORA_COLS), lambda b, i: (b, i, COL_L // LORA_COLS)),
    ]
    pt = 1 if carry else tb
    pidx = (lambda b, i: 0) if carry else (lambda b, i: i)
    p_specs = [
        pl.BlockSpec((1, pt, WIDTH), lambda b, i: (b, pidx(b, i), 0)),
        pl.BlockSpec((1, pt, WIDTH), lambda b, i: (b, pidx(b, i), 1)),
        pl.BlockSpec((1, pt, WIDTH), lambda b, i: (b, pidx(b, i), 2)),
        pl.BlockSpec((1, pt, LORA_COLS), lambda b, i: (b, pidx(b, i), 3 * WIDTH // LORA_COLS)),
    ]
    const = lambda shape: pl.BlockSpec(shape, lambda b, i: (0,) * len(shape))
    w_specs = [const((1, RWKV_COLS)), const((1, WIDTH)), const((1, WIDTH)), const((1, WIDTH)),
               const((1, WIDTH)), const((1, WIDTH)), const((LANES, 2 * WIDTH)),
               const((LANES, WIDTH)), const((LANES, LANES))]
    row_spec = pl.BlockSpec((1, tb, WIDTH), lambda b, i: (b, i, 0))
    pair_spec = pl.BlockSpec((1, N_PAIRS, tb, LANES), lambda b, i: (b, 0, i, 0))
    out_specs = [pair_spec] * N_SCAN_SEQS + [row_spec] * 2
    out_shape = ([jax.ShapeDtypeStruct((bsz, N_PAIRS, t, LANES), F32)] * N_SCAN_SEQS
                 + [jax.ShapeDtypeStruct((bsz, t, WIDTH), F32)] * 2)
    scratch =([pltpu.VMEM((1, WIDTH), F32)] * 3 + [pltpu.VMEM((1, LORA_COLS), F32)]) if carry else []
    return pl.pallas_call(
        _pre_carry_kernel if carry else _pre_given_kernel,
        out_shape=out_shape,
        grid=(bsz, nt),
        in_specs=z_specs + p_specs + w_specs,
        out_specs=out_specs,
        scratch_shapes=scratch,
        compiler_params=_cparams(("parallel", "arbitrary")),
        name="rwkv_prelude",
    )(z3, z3, z3, z3, prev3, prev3, prev3, prev3, *params)


def _scan_kernel(r_ref, w_ref, k_ref, v_ref, kn_ref, b_ref, s0_ref, mask_ref, bd_ref, bdd_ref,
                 o_ref, sout_ref, s_ref, x_ref, *, bb, tt):
    ti = pl.program_id(1)

    @pl.when(ti == 0)
    def _():
        s_ref[...] = s0_ref[...]

    maskf = mask_ref[...]
    bd = bd_ref[...]
    bdd = bdd_ref[...]

    def row(ref, bi, t):
        return jnp.concatenate(
            [jnp.broadcast_to(ref[bi, p, pl.ds(t, 1), :], (HEAD_DIM, LANES))
             for p in range(N_PAIRS)], axis=0)

    def v_diag(bi, t):
        return (maskf * row(v_ref, bi, t)).astype(BF16)

    def kk_operand(bi, t):
        return (s_ref[bi] * row(kn_ref, bi, t)).astype(BF16)

    def advance(bi, t, sa, vt):
        s_new = s_ref[bi] * row(w_ref, bi, t) - sa * row(b_ref, bi, t) + vt * row(k_ref, bi, t)
        s_ref[bi] = s_new
        return (s_new * row(r_ref, bi, t)).astype(BF16)

    def store_o(bi, t, of):
        of = maskf * of
        for p in range(N_PAIRS):
            o_ref[bi, p, pl.ds(t, 1), :] = jnp.sum(of[p * HEAD_DIM:(p + 1) * HEAD_DIM],
                                                   axis=0, keepdims=True)

    seqs = range(bb)
    if tt == 1:
        for bi in seqs:
            both = _dot(jnp.concatenate([kk_operand(bi, 0), v_diag(bi, 0)], axis=1), bdd)
            x = advance(bi, 0, both[:, :LANES], both[:, LANES:])
            store_o(bi, 0, _dot(x, bd))
    else:
        for bi in seqs:
            x_ref[bi] = jnp.zeros_like(x_ref[bi])

        def two_steps(i, carry):
            t0 = 2 * i
            vts = [_dot(jnp.concatenate([v_diag(bi, t0), v_diag(bi, t0 + 1)], axis=1), bdd)
                   for bi in seqs]
            for half in range(2):
                t = t0 + half
                comb = [_dot(jnp.concatenate([kk_operand(bi, t), x_ref[bi]], axis=1), bdd)
                        for bi in seqs]
                for bi in seqs:
                    x_ref[bi] = advance(bi, t, comb[bi][:, :LANES],
                                        vts[bi][:, half * LANES:(half + 1) * LANES])
                for bi in seqs:
                    store_o(bi, jnp.maximum(t - 1, 0), comb[bi][:, LANES:])
            return carry

        lax.fori_loop(0, tt // 2, two_steps, 0, unroll=4)
        for bi in seqs:
            store_o(bi, tt - 1, _dot(x_ref[bi], bd))

    @pl.when(ti == pl.num_programs(1) - 1)
    def _():
        sout_ref[...] = s_ref[...]


def _rwkv_scan(r, w, k, v, kn, b, s0, consts, tt):
    bsz, _, t, _ = r.shape
    bb = 2
    rows = N_PAIRS * HEAD_DIM
    seq_spec = pl.BlockSpec((bb, N_PAIRS, tt, LANES), lambda bi, ti: (bi, 0, ti, 0))
    st_spec = pl.BlockSpec((bb, rows, LANES), lambda bi, ti: (bi, 0, 0))
    const = lambda shape: pl.BlockSpec(shape, lambda bi, ti: (0,) * len(shape))
    return pl.pallas_call(
        functools.partial(_scan_kernel, bb=bb, tt=tt),
        out_shape=[jax.ShapeDtypeStruct((bsz, N_PAIRS, t, LANES), F32),
                   jax.ShapeDtypeStruct((bsz, rows, LANES), F32)],
        grid=(bsz // bb, t // tt),
        in_specs=[seq_spec] * 6 + [st_spec, const((rows, LANES)), const((LANES, LANES)),
                                   const((2 * LANES, 2 * LANES))],
        out_specs=[seq_spec, st_spec],
        scratch_shapes=[pltpu.VMEM((bb, rows, LANES), F32), pltpu.VMEM((bb, rows, LANES), BF16)],
        compiler_params=_cparams(("parallel", "arbitrary")),
        name="rwkv_scan",
    )(r, w, k, v, kn, b, s0, *consts)


def _pairs_from_heads(s):
    bsz = s.shape[0]
    s = s.reshape(bsz, N_PAIRS, 2, HEAD_DIM, HEAD_DIM).transpose(0, 1, 3, 2, 4)
    return s.reshape(bsz, N_PAIRS * HEAD_DIM, LANES)


def _heads_from_pairs(s):
    bsz = s.shape[0]
    s = s.reshape(bsz, N_PAIRS, HEAD_DIM, 2, HEAD_DIM).transpose(0, 1, 3, 2, 4)
    return s.reshape(bsz, N_HEADS, HEAD_DIM, HEAD_DIM)


def _post_kernel(o_ref, bo_ref, g_ref, lg_ref, lb_ref, bd_ref, out_ref):
    bd = bd_ref[...]
    o = jnp.concatenate([o_ref[0, p] for p in range(N_PAIRS)], axis=1)
    mean = _seg_sum(o, bd) * (1.0 / HEAD_DIM)
    d = o - mean
    var = _seg_sum(d * d, bd) * (1.0 / HEAD_DIM)
    on = d * lax.rsqrt(var + GN_EPS) * lg_ref[...] + lb_ref[...]
    out_ref[0] = ((on + bo_ref[0]) * g_ref[0]).astype(BF16)


def _rwkv_post(o, bonus, g, lnx_g, lnx_b, bd, tb):
    bsz, _, t, _ = o.shape
    spec = pl.BlockSpec((1, tb, WIDTH), lambda b, i: (b, i, 0))
    o_spec = pl.BlockSpec((1, N_PAIRS, tb, LANES), lambda b, i: (b, 0, i, 0))
    const = lambda shape: pl.BlockSpec(shape, lambda b, i: (0,) * len(shape))
    return pl.pallas_call(
        _post_kernel,
        out_shape=jax.ShapeDtypeStruct((bsz, t, WIDTH), BF16),
        grid=(bsz, t // tb),
        in_specs=[o_spec, spec, spec, const((1, WIDTH)), const((1, WIDTH)), const((LANES, LANES))],
        out_specs=spec,
        compiler_params=_cparams(("parallel", "parallel")),
        name="rwkv_post",
    )(o, bonus, g, lnx_g, lnx_b, bd)


def _fox_norm(x, gain, bd):
    ms = _seg_sum(x * x, bd) * (1.0 / HEAD_DIM)
    return x * lax.rsqrt(ms + NORM_EPS) * gain


def _log_sigmoid(x):
    return jnp.minimum(x, 0.0) - jnp.log(1.0 + jnp.exp(-jnp.abs(x)))


def _fox_prep_kernel(fq_ref, fk_ref, fv_ref, fl_ref, bf_ref, qg_ref, kg_ref, bd_ref, tril_ref,
                     pq_ref, pk_ref, kn_o, v_o, lf_o, qa_o, ka_o, va_o, c_ref):
    bd = bd_ref[...]
    q = _fox_norm(fq_ref[0], qg_ref[...], bd) * ATTN_SCALE
    kn = _fox_norm(fk_ref[0], kg_ref[...], bd)
    v = fv_ref[0]
    kn_o[0] = kn
    v_o[0] = v
    lane = lax.broadcasted_iota(jnp.int32, fl_ref[0].shape, 1)
    lf = jnp.where(lane < N_HEADS, _log_sigmoid(fl_ref[0] + bf_ref[...]), 0.0)
    lf_o[0] = lf

    @pl.when(pl.program_id(1) == 0)
    def _():
        c_ref[...] = jnp.zeros_like(c_ref)

    tril = tril_ref[...]
    h1, h2, h3 = _split3(lf)
    c = _dot(tril, h1) + _dot(tril, h2) + _dot(tril, h3) + c_ref[...]
    n = c.shape[0]
    c_ref[...] = c[n - 1:n, :]
    c1, c2, c3 = _split3(c)
    cpack = (c1.astype(F32) + pltpu.roll(c2.astype(F32), N_HEADS, 1)
             + pltpu.roll(c3.astype(F32), 2 * N_HEADS, 1)).astype(BF16)
    lane_v = lax.broadcasted_iota(jnp.int32, (n, LANES), 1)
    ones_q = jnp.where((lane_v >= HEAD_DIM + 3) & (lane_v < HEAD_DIM + 6), 1.0, 0.0)
    ones_k = jnp.where((lane_v >= HEAD_DIM) & (lane_v < HEAD_DIM + 3), 1.0, 0.0)
    for p in range(N_PAIRS):
        sl = slice(p * LANES, (p + 1) * LANES)
        lhs_q = jnp.concatenate([q[:, sl].astype(BF16), cpack], axis=1)
        lhs_k = jnp.concatenate([kn[:, sl].astype(BF16), cpack], axis=1)
        vs = v[:, sl]
        for hh in range(2):
            h = 2 * p + hh
            qa_o[0, h] = (_dot(lhs_q, pq_ref[h]) + ones_q).astype(BF16)
            ka_o[0, h] = (_dot(lhs_k, pk_ref[h]) + ones_k).astype(BF16)
            if hh == 0:
                va = jnp.where(lane_v < HEAD_DIM, vs, jnp.where(lane_v == HEAD_DIM, 1.0, 0.0))
            else:
                va = jnp.where(lane_v >= HEAD_DIM, vs, jnp.where(lane_v == 0, 1.0, 0.0))
            va_o[0, h] = va.astype(BF16)


def _fox_prep_plain_kernel(fq_ref, fk_ref, fv_ref, fl_ref, bf_ref, qg_ref, kg_ref, bd_ref,
                           kn_o, v_o, lf_o, q_o):
    bd = bd_ref[...]
    q_o[0] = _fox_norm(fq_ref[0], qg_ref[...], bd) * ATTN_SCALE
    kn_o[0] = _fox_norm(fk_ref[0], kg_ref[...], bd)
    v_o[0] = fv_ref[0]
    lane = lax.broadcasted_iota(jnp.int32, fl_ref[0].shape, 1)
    lf_o[0] = jnp.where(lane < N_HEADS, _log_sigmoid(fl_ref[0] + bf_ref[...]), 0.0)


def _fox_prep(z3, params, tb, augment):
    bsz, t, _ = z3.shape
    z_specs = [
        pl.BlockSpec((1, tb, WIDTH), lambda b, i: (b, i, COL_FQ // WIDTH)),
        pl.BlockSpec((1, tb, WIDTH), lambda b, i: (b, i, COL_FK // WIDTH)),
        pl.BlockSpec((1, tb, WIDTH), lambda b, i: (b, i, COL_FV // WIDTH)),
        pl.BlockSpec((1, tb, LANES), lambda b, i: (b, i, COL_FL // LANES)),
    ]
    const = lambda shape: pl.BlockSpec(shape, lambda b, i: (0,) * len(shape))
    w_specs = [const((1, LANES)), const((1, WIDTH)), const((1, WIDTH)), const((LANES, LANES))]
    row_spec = pl.BlockSpec((1, tb, WIDTH), lambda b, i: (b, i, 0))
    lf_spec = pl.BlockSpec((1, tb, LANES), lambda b, i: (b, i, 0))
    row_shape = jax.ShapeDtypeStruct((bsz, t, WIDTH), F32)
    lf_shape = jax.ShapeDtypeStruct((bsz, t, LANES), F32)
    if not augment:
        return pl.pallas_call(
            _fox_prep_plain_kernel,
            out_shape=[row_shape, row_shape, lf_shape, row_shape],
            grid=(bsz, t // tb),
            in_specs=z_specs + w_specs,
            out_specs=[row_spec, row_spec, lf_spec, row_spec],
            compiler_params=_cparams(("parallel", "arbitrary")),
            name="fox_prep_sample",
        )(z3, z3, z3, z3, *params[:4])
    aug_spec = pl.BlockSpec((1, N_HEADS, tb, LANES), lambda b, i: (b, 0, i, 0))
    aug_shape = jax.ShapeDtypeStruct((bsz, N_HEADS, t, LANES), BF16)
    w_specs += [const((tb, tb)), const((N_HEADS, 2 * LANES, LANES)),
                const((N_HEADS, 2 * LANES, LANES))]
    return pl.pallas_call(
        _fox_prep_kernel,
        out_shape=[row_shape, row_shape, lf_shape, aug_shape, aug_shape, aug_shape],
        grid=(bsz, t // tb),
        in_specs=z_specs + w_specs,
        out_specs=[row_spec, row_spec, lf_spec, aug_spec, aug_spec, aug_spec],
        scratch_shapes=[pltpu.VMEM((1, LANES), F32)],
        compiler_params=_cparams(("parallel", "arbitrary")),
        name="fox_prep",
    )(z3, z3, z3, z3, *params)


def _place_matrices():
    h = jnp.arange(N_HEADS)[:, None, None]
    row = jnp.arange(2 * LANES)[None, :, None]
    col = jnp.arange(LANES)[None, None, :]
    data = (row < LANES) & (row == (h % 2) * HEAD_DIM + col) & (col < HEAD_DIM)
    piece = (row - LANES) // N_HEADS
    is_c = (row >= LANES) & (row < LANES + 3 * N_HEADS) & ((row - LANES) % N_HEADS == h)
    cq = is_c & (col == HEAD_DIM + piece)
    ck = is_c & (col == HEAD_DIM + 3 + piece)
    pq = (data | cq).astype(F32)
    pk = data.astype(F32) - ck.astype(F32)
    return pq.astype(BF16), pk.astype(BF16)


def _flash_kernel(q_ref, k_ref, v_ref, og_ref, o_ref, *, tq, tk, nh):
    qi = pl.program_id(2)
    n_full = (qi * tq) // tk
    row = lax.broadcasted_iota(jnp.int32, (tq, tk), 0) + qi * tq
    col = lax.broadcasted_iota(jnp.int32, (tq, tk), 1) + n_full * tk
    causal = col <= row
    qs = [q_ref[0, hh] for hh in range(nh)]

    def block(j, carry, masked):
        start = pl.multiple_of(j * tk, tk)
        new = []
        for hh in range(nh):
            m, acc = carry[2 * hh], carry[2 * hh + 1]
            kb = k_ref[0, hh, pl.ds(start, tk), :]
            vb = v_ref[0, hh, pl.ds(start, tk), :]
            s = lax.dot_general(qs[hh], kb, (((1,), (1,)), ((), ())), preferred_element_type=F32)
            if masked:
                s = jnp.where(causal, s, NEG_BIG)
            m_new = jnp.maximum(m, jnp.max(s, axis=1, keepdims=True))
            p = jnp.exp(s - m_new)
            new += [m_new, jnp.exp(m - m_new) * acc + _dot(p.astype(BF16), vb)]
        return tuple(new)

    init = (jnp.full((tq, 1), NEG_BIG, F32), jnp.zeros((tq, LANES), F32)) * nh
    carry = lax.fori_loop(0, n_full, lambda j, c: block(j, c, False), init)
    carry = block(n_full, carry, True)
    lane = lax.broadcasted_iota(jnp.int32, (tq, LANES), 1)
    for pp in range(nh // 2):
        acc_e, acc_o = carry[4 * pp + 1], carry[4 * pp + 3]
        o = jnp.where(lane < HEAD_DIM, acc_e / acc_e[:, HEAD_DIM:HEAD_DIM + 1],
                      acc_o / acc_o[:, 0:1])
        sl = slice(pp * LANES, (pp + 1) * LANES)
        o_ref[0, :, sl] = (o * _sigmoid(og_ref[0, :, sl])).astype(BF16)


def _fox_flash(qa, ka, va, z3, tq, tk, nh):
    bsz, _, t, _ = qa.shape
    gw = (nh // 2) * LANES
    return pl.pallas_call(
        functools.partial(_flash_kernel, tq=tq, tk=tk, nh=nh),
        out_shape=jax.ShapeDtypeStruct((bsz, t, WIDTH), BF16),
        grid=(bsz, N_HEADS // nh, t // tq),
        in_specs=[
            pl.BlockSpec((1, nh, tq, LANES), lambda b, g, i: (b, g, i, 0)),
            pl.BlockSpec((1, nh, t, LANES), lambda b, g, i: (b, g, 0, 0)),
            pl.BlockSpec((1, nh, t, LANES), lambda b, g, i: (b, g, 0, 0)),
            pl.BlockSpec((1, tq, gw), lambda b, g, i: (b, i, COL_OG // gw + g)),
        ],
        out_specs=pl.BlockSpec((1, tq, gw), lambda b, g, i: (b, i, g)),
        compiler_params=_cparams(("parallel", "parallel", "arbitrary")),
        name="fox_flash",
    )(qa, ka, va, z3)


def _decode_kernel(pt_ref, qt_ref, kc_ref, vc_ref, lc_ref, og_ref, lt_ref, on_ref, *rest, npg):
    k_refs = rest[0:npg]
    v_refs = rest[npg:2 * npg]
    l_refs = rest[2 * npg:3 * npg]
    o_ref = rest[3 * npg]
    m_ref, l_ref, cy_ref, acc_ref = rest[3 * npg + 1:]
    step = pl.program_id(1)
    later = lt_ref[...]
    ones = on_ref[...]

    def rows(fn):
        return jnp.concatenate([fn(h) for h in range(N_HEADS)], axis=0)

    @pl.when(step == 0)
    def _():
        m_ref[...] = rows(lambda h: jnp.sum(qt_ref[0, h] * kc_ref[0, h], axis=0, keepdims=True))
        l_ref[...] = jnp.ones_like(l_ref)
        cy_ref[...] = lc_ref[0]
        lane = lax.broadcasted_iota(jnp.int32, (HEAD_DIM, LANES), 1)
        for h in range(N_HEADS):
            acc_ref[h] = jnp.where(lane == 0, vc_ref[0, h], 0.0)

    for i in range(npg):
        kp, vp = k_refs[i], v_refs[i]
        lf = l_refs[i][0]
        h1, h2, h3 = _split3(lf)
        bias = _dot(h1, later) + _dot(h2, later) + _dot(h3, later) + cy_ref[...]
        cy_ref[...] = cy_ref[...] + _dot(h1, ones) + _dot(h2, ones) + _dot(h3, ones)
        s = rows(lambda h: jnp.sum(qt_ref[0, h] * kp[0, h], axis=0, keepdims=True)) + bias
        m_old = m_ref[...]
        m_new = jnp.maximum(m_old, jnp.max(s, axis=1, keepdims=True))
        alpha = jnp.exp(m_old - m_new)
        p = jnp.exp(s - m_new)
        m_ref[...] = m_new
        l_ref[...] = alpha * l_ref[...] + jnp.sum(p, axis=1, keepdims=True)
        for h in range(N_HEADS):
            acc_ref[h] = alpha[h:h + 1, :] * acc_ref[h] + p[h:h + 1, :] * vp[0, h]

    @pl.when(step == pl.num_programs(1) - 1)
    def _():
        ones8 = ones[0:8, :]

        def token_sum(h):
            nt = lambda a: lax.dot_general(ones8, a, (((1,), (1,)), ((), ())),
                                           preferred_element_type=F32)
            a1, a2, a3 = _split3(acc_ref[h])
            return (nt(a1) + nt(a2) + nt(a3))[0:1, :]

        o_ref[0] = (rows(token_sum) / l_ref[:, 0:HEAD_DIM]) * _sigmoid(og_ref[0])


def _fox_decode(page_table, qt, kc, vc, lfc, og, cache_kt, cache_vt, cache_lt, later, ones, npg):
    dbsz, n_pages = page_table.shape
    page = cache_kt.shape[-1]
    steps = n_pages // npg

    def page_map(i, nd):
        return lambda b, s, pt: (pt[b, n_pages - 1 - (s * npg + i)],) + (0,) * nd

    tile_spec = pl.BlockSpec((1, N_HEADS, HEAD_DIM, page), lambda b, s, pt: (b, 0, 0, 0))
    in_specs = [
        tile_spec, tile_spec, tile_spec,
        pl.BlockSpec((1, N_HEADS, page), lambda b, s, pt: (b, 0, 0)),
        pl.BlockSpec((1, N_HEADS, HEAD_DIM), lambda b, s, pt: (b, 0, 0)),
        pl.BlockSpec((page, page), lambda b, s, pt: (0, 0)),
        pl.BlockSpec((page, page), lambda b, s, pt: (0, 0)),
    ]
    in_specs += [pl.BlockSpec((1, N_HEADS, HEAD_DIM, page), page_map(i, 3)) for i in range(npg)]
    in_specs += [pl.BlockSpec((1, N_HEADS, HEAD_DIM, page), page_map(i, 3)) for i in range(npg)]
    in_specs += [pl.BlockSpec((1, N_HEADS, page), page_map(i, 2)) for i in range(npg)]
    stat = pltpu.VMEM((N_HEADS, page), F32)
    grid_spec = pltpu.PrefetchScalarGridSpec(
        num_scalar_prefetch=1,
        grid=(dbsz, steps),
        in_specs=in_specs,
        out_specs=pl.BlockSpec((1, N_HEADS, HEAD_DIM), lambda b, s, pt: (b, 0, 0)),
        scratch_shapes=[stat, stat, stat, pltpu.VMEM((N_HEADS, HEAD_DIM, page), F32)],
    )
    return pl.pallas_call(
        functools.partial(_decode_kernel, npg=npg),
        out_shape=jax.ShapeDtypeStruct((dbsz, N_HEADS, HEAD_DIM), F32),
        grid_spec=grid_spec,
        compiler_params=_cparams(("parallel", "arbitrary")),
        name="fox_decode",
    )(page_table, qt, kc, vc, lfc, og, later, ones,
      *([cache_kt] * npg), *([cache_vt] * npg), *([cache_lt] * npg))


def _out_proj_kernel(x_ref, oa_ref, ob_ref, wa_ref, wb_ref, h_ref):
    h_ref[...] = x_ref[...] + _dot(oa_ref[...], wa_ref[...]) + _dot(ob_ref[...], wb_ref[...])


def _out_proj(x, oa, ob, w_out, tm, tn):
    m, d = x.shape
    return pl.pallas_call(
        _out_proj_kernel,
        out_shape=jax.ShapeDtypeStruct((m, d), F32),
        grid=(m // tm, d // tn),
        in_specs=[
            pl.BlockSpec((tm, tn), lambda i, j: (i, j)),
            pl.BlockSpec((tm, WIDTH), lambda i, j: (i, 0)),
            pl.BlockSpec((tm, WIDTH), lambda i, j: (i, 0)),
            pl.BlockSpec((WIDTH, tn), lambda i, j: (0, j)),
            pl.BlockSpec((WIDTH, tn), lambda i, j: (1, j)),
        ],
        out_specs=pl.BlockSpec((tm, tn), lambda i, j: (i, j)),
        compiler_params=_cparams(("parallel", "arbitrary")),
        name="out_proj",
    )(x, oa, ob, w_out, w_out)


def _ffn_kernel(h_ref, gn_ref, wg_ref, wu_ref, wd_ref, gf_ref, y_ref, hn_ref, acc_ref):
    f = pl.program_id(1)

    @pl.when(f == 0)
    def _():
        h = h_ref[...]
        ms = jnp.mean(h * h, axis=-1, keepdims=True)
        hn_ref[...] = (h * lax.rsqrt(ms + NORM_EPS) * gn_ref[...]).astype(BF16)
        acc_ref[...] = jnp.zeros_like(acc_ref)

    hn = hn_ref[...]
    gate = _dot(hn, wg_ref[...])
    up = _dot(hn, wu_ref[...])
    act = (gate * _sigmoid(gate) * up).astype(BF16)
    acc_ref[...] += _dot(act, wd_ref[...])

    @pl.when(f == pl.num_programs(1) - 1)
    def _():
        hh = h_ref[...] + acc_ref[...]
        ms = jnp.mean(hh * hh, axis=-1, keepdims=True)
        y_ref[...] = hh * lax.rsqrt(ms + NORM_EPS) * gf_ref[...]


def _ffn(h, g_ffn, wg, wu, wd, g_final, tm, tf):
    m, d = h.shape
    dff = wg.shape[1]
    return pl.pallas_call(
        _ffn_kernel,
        out_shape=jax.ShapeDtypeStruct((m, d), F32),
        grid=(m // tm, dff // tf),
        in_specs=[
            pl.BlockSpec((tm, d), lambda i, f: (i, 0)),
            pl.BlockSpec((1, d), lambda i, f: (0, 0)),
            pl.BlockSpec((d, tf), lambda i, f: (0, f)),
            pl.BlockSpec((d, tf), lambda i, f: (0, f)),
            pl.BlockSpec((tf, d), lambda i, f: (f, 0)),
            pl.BlockSpec((1, d), lambda i, f: (0, 0)),
        ],
        out_specs=pl.BlockSpec((tm, d), lambda i, f: (i, 0)),
        scratch_shapes=[pltpu.VMEM((tm, d), BF16), pltpu.VMEM((tm, d), F32)],
        compiler_params=_cparams(("parallel", "arbitrary")),
        name="ffn",
    )(h, g_ffn, wg, wu, wd, g_final)


def _block_diag_ones():
    i = jnp.arange(LANES)
    return (i[:, None] // HEAD_DIM == i[None, :] // HEAD_DIM).astype(BF16)


def _pick(n, cap):
    t = min(n, cap)
    while n % t:
        t //= 2
    return t


def kernel(x_prompt, x_sample, cache_k, cache_v, cache_logf, state_wkv, state_shift, page_table,
           norm_mix, w_in, mu_shift, w0, w_decay_up, a0, w_aaa_up, w_gate_up, k_k, k_a, r_k,
           lnx_g, lnx_b, b_f, q_norm, k_norm, w_out, norm_ffn, w_ffn_gate, w_ffn_up, w_ffn_down,
           norm_final):
    bsz, seq, d = x_prompt.shape
    dbsz = x_sample.shape[0]
    lyr = 0
    rw = RWKV_COLS

    wi = w_in[lyr]
    w_z = jnp.concatenate(
        [wi[:, rw:rw + 4 * WIDTH], wi[:, :rw], wi[:, rw + 4 * WIDTH:],
         jnp.zeros((d, Z_COLS - N_Z_USED), F32)], axis=1).astype(BF16)
    g_mix = norm_mix[lyr][None, :]
    wda = jnp.zeros((LANES, 2 * WIDTH), F32)
    wda = wda.at[:HEAD_DIM, :WIDTH].set(w_decay_up[lyr]).at[HEAD_DIM:, WIDTH:].set(w_aaa_up[lyr])
    bd = _block_diag_ones()
    bd2 = jnp.concatenate([bd, bd], axis=0)
    pre_params = (mu_shift[lyr][None, :], w0[lyr][None, :], a0[lyr][None, :], k_k[lyr][None, :],
                  k_a[lyr][None, :], r_k[lyr].reshape(1, WIDTH), wda.astype(BF16),
                  w_gate_up[lyr].astype(BF16), bd)
    li = jnp.arange(LANES)
    maskf = (li[None, :] % HEAD_DIM
             == jnp.arange(N_PAIRS * HEAD_DIM)[:, None] % HEAD_DIM).astype(F32)
    zb = jnp.zeros_like(bd)
    bdd = jnp.concatenate([jnp.concatenate([bd, zb], axis=1),
                           jnp.concatenate([zb, bd], axis=1)], axis=0)
    scan_consts = (maskf, bd, bdd)
    lg, lb = lnx_g[lyr][None, :], lnx_b[lyr][None, :]
    bfp = jnp.zeros((1, LANES), F32).at[0, :N_HEADS].set(b_f[lyr])
    qg = jnp.tile(q_norm[lyr], N_HEADS)[None, :]
    kg = jnp.tile(k_norm[lyr], N_HEADS)[None, :]
    tb_fox = _pick(seq, 256)
    tril = (jnp.arange(tb_fox)[:, None] >= jnp.arange(tb_fox)[None, :]).astype(BF16)
    pq, pk = _place_matrices()
    w_o = w_out[lyr].astype(BF16)
    wg = w_ffn_gate[lyr].astype(BF16)
    wu = w_ffn_up[lyr].astype(BF16)
    wd = w_ffn_down[lyr].astype(BF16)
    g_ffn = norm_ffn[lyr][None, :]
    g_fin = norm_final[None, :]

    m_p = bsz * seq
    xp = x_prompt.reshape(m_p, d)
    z_p = _norm_matmul(xp, g_mix, w_z, _pick(m_p, 1024), 1536)
    z3p = z_p.reshape(bsz, seq, Z_COLS)
    first = jnp.zeros((bsz, 1, rw), F32)
    r, wdec, k2, v, kn, bb_, bonus, gate = _rwkv_prelude(z3p, first, pre_params,
                                                        _pick(seq, 256), True)
    s0 = jnp.zeros((bsz, N_PAIRS * HEAD_DIM, LANES), F32)
    o_p, s_p = _rwkv_scan(r, wdec, k2, v, kn, bb_, s0, scan_consts, _pick(seq, 128))
    oa_p = _rwkv_post(o_p, bonus, gate, lg, lb, bd, _pick(seq, 512))
    fkn_p, fv_p, lf_p, qa, ka, va = _fox_prep(z3p, (bfp, qg, kg, bd, tril, pq, pk), tb_fox, True)
    ob_p = _fox_flash(qa, ka, va, z3p, _pick(seq, 1024), _pick(seq, 1024), 2)
    h_p = _out_proj(xp, oa_p.reshape(m_p, WIDTH), ob_p.reshape(m_p, WIDTH), w_o,
                    _pick(m_p, 1024), 512)
    y_p = _ffn(h_p, g_ffn, wg, wu, wd, g_fin, _pick(m_p, 512), 512)

    xs = x_sample.reshape(dbsz, d)
    z_s = _norm_matmul(xs, g_mix, w_z, dbsz, 512)
    z3s = z_s.reshape(1, dbsz, Z_COLS)
    prev_s = state_shift[lyr].reshape(1, dbsz, rw)
    outs = _rwkv_prelude(z3s, prev_s, pre_params, dbsz, False)
    seq_major = lambda a: a.transpose(2, 1, 0, 3)
    r, wdec, k2, v, kn, bb_ = [seq_major(a) for a in outs[:N_SCAN_SEQS]]
    bonus, gate = outs[N_SCAN_SEQS:]
    o_s, s_s = _rwkv_scan(r, wdec, k2, v, kn, bb_, _pairs_from_heads(state_wkv[lyr]),
                          scan_consts, 1)
    oa_s = _rwkv_post(seq_major(o_s), bonus, gate, lg, lb, bd, dbsz)
    fkn_s, fv_s, lf_s, q_s = _fox_prep(z3s, (bfp, qg, kg, bd), dbsz, False)
    page = cache_k.shape[2]
    lane_tile = lambda a: jnp.broadcast_to(a.reshape(dbsz, N_HEADS, HEAD_DIM, 1),
                                           (dbsz, N_HEADS, HEAD_DIM, page))
    lfc = jnp.broadcast_to(lf_s[0, :, :N_HEADS, None], (dbsz, N_HEADS, page))
    og_s = z_s[:, COL_OG:COL_OG + WIDTH].reshape(dbsz, N_HEADS, HEAD_DIM)
    later = (jnp.arange(page)[:, None] > jnp.arange(page)[None, :]).astype(BF16)
    ob_s = _fox_decode(page_table, lane_tile(q_s), lane_tile(fkn_s), lane_tile(fv_s), lfc, og_s,
                       cache_k[lyr].transpose(0, 2, 3, 1), cache_v[lyr].transpose(0, 2, 3, 1),
                       cache_logf[lyr].transpose(0, 2, 1), later, jnp.ones((page, page), BF16),
                       _pick(page_table.shape[1], 8))
    h_s = _out_proj(xs, oa_s.reshape(dbsz, WIDTH), ob_s.reshape(dbsz, WIDTH).astype(BF16),
                    w_o, dbsz, 512)
    y_s = _ffn(h_s, g_ffn, wg, wu, wd, g_fin, dbsz, 512)

    hd = (N_HEADS, HEAD_DIM)
    return (
        y_p.reshape(bsz, seq, d),
        y_s.reshape(dbsz, 1, d),
        fkn_p.reshape(1, bsz, seq, *hd),
        fv_p.reshape(1, bsz, seq, *hd),
        lf_p[:, :, :N_HEADS].reshape(1, bsz, seq, N_HEADS),
        _heads_from_pairs(s_p)[None],
        z3p[:, seq - 1, COL_R:COL_R + rw][None],
        fkn_s.reshape(1, dbsz, 1, *hd),
        fv_s.reshape(1, dbsz, 1, *hd),
        lf_s[0, :, :N_HEADS].reshape(1, dbsz, 1, N_HEADS),
        _heads_from_pairs(s_s)[None],
        z_s[:, COL_R:COL_R + rw][None],
    )
```

```python
import functools

import jax
import jax.numpy as jnp
from jax import lax
from jax.experimental import pallas as pl
from jax.experimental.pallas import tpu as pltpu

F32 = jnp.float32
BF16 = jnp.bfloat16

HEAD_DIM = 64
N_HEADS = 16
WIDTH = N_HEADS * HEAD_DIM
N_PAIRS = N_HEADS // 2
LANES = 128
LORA_COLS = 256
RWKV_COLS = 3 * WIDTH + LORA_COLS
NORM_EPS = 1e-6
GN_EPS = 64e-5
ATTN_SCALE = HEAD_DIM ** -0.5
EXP_NEG_HALF = 0.6065306597126334
NEG_BIG = -1e30

Z_COLS = 7680
COL_FQ, COL_FK, COL_FV, COL_OG = 0, 1024, 2048, 3072
COL_R, COL_K, COL_V, COL_L = 4096, 5120, 6144, 7168
COL_FL = 7424
N_Z_USED = 7440

VMEM_LIMIT = 56 * 1024 * 1024


def _cparams(sem):
    return pltpu.CompilerParams(dimension_semantics=sem, vmem_limit_bytes=VMEM_LIMIT)


def _sigmoid(x):
    return 1.0 / (1.0 + jnp.exp(-x))


def _dot(a, b):
    return jnp.dot(a, b, preferred_element_type=F32)


def _split2(x):
    hi = x.astype(BF16)
    lo = (x - hi.astype(F32)).astype(BF16)
    return hi, lo


def _split3(x):
    h1 = x.astype(BF16)
    r1 = x - h1.astype(F32)
    h2 = r1.astype(BF16)
    h3 = (r1 - h2.astype(F32)).astype(BF16)
    return h1, h2, h3


def _seg_sum(x, bd):
    parts = []
    for c in range(x.shape[1] // LANES):
        hi, lo = _split2(x[:, c * LANES:(c + 1) * LANES])
        parts.append(_dot(hi, bd) + _dot(lo, bd))
    return parts[0] if len(parts) == 1 else jnp.concatenate(parts, axis=1)


def _norm_matmul_kernel(x_ref, g_ref, w_ref, o_ref, xn_ref):
    @pl.when(pl.program_id(1) == 0)
    def _():
        x = x_ref[...]
        ms = jnp.mean(x * x, axis=-1, keepdims=True)
        xn_ref[...] = (x * lax.rsqrt(ms + NORM_EPS) * g_ref[...]).astype(BF16)

    o_ref[...] = _dot(xn_ref[...], w_ref[...])


def _norm_matmul(x, g, w, tm, tn):
    m, d = x.shape
    n = w.shape[1]
    return pl.pallas_call(
        _norm_matmul_kernel,
        out_shape=jax.ShapeDtypeStruct((m, n), F32),
        grid=(m // tm, n // tn),
        in_specs=[
            pl.BlockSpec((tm, d), lambda i, j: (i, 0)),
            pl.BlockSpec((1, d), lambda i, j: (0, 0)),
            pl.BlockSpec((d, tn), lambda i, j: (0, j)),
        ],
        out_specs=pl.BlockSpec((tm, tn), lambda i, j: (i, j)),
        scratch_shapes=[pltpu.VMEM((tm, d), BF16)],
        compiler_params=_cparams(("parallel", "arbitrary")),
        name="in_proj",
    )(x, g, w)


def _pre_math(zr, zk, zv, zl, pr, pk, pv, plo, mu_ref, w0_ref, a0_ref, kk_ref, ka_ref, rk_ref,
              wda_ref, wg_ref, bd_ref):
    bd = bd_ref[...]
    r = zr + (pr - zr) * mu_ref[:, 0:WIDTH]
    k = zk + (pk - zk) * mu_ref[:, WIDTH:2 * WIDTH]
    v = zv + (pv - zv) * mu_ref[:, 2 * WIDTH:3 * WIDTH]
    lo = zl + (plo - zl) * mu_ref[:, 3 * WIDTH:3 * WIDTH + LORA_COLS]
    l0 = lo[:, 0:LANES]
    lane = lax.broadcasted_iota(jnp.int32, l0.shape, 1)
    lhs = jnp.where(lane < HEAD_DIM, jnp.tanh(l0), l0).astype(BF16)
    da = _dot(lhs, wda_ref[...])
    decay = jnp.exp(-EXP_NEG_HALF * _sigmoid(w0_ref[...] + da[:, 0:WIDTH]))
    a = _sigmoid(a0_ref[...] + da[:, WIDTH:2 * WIDTH])
    g = _dot(_sigmoid(lo[:, LANES:2 * LANES]).astype(BF16), wg_ref[...])
    kk = k * kk_ref[...]
    nrm = jnp.sqrt(_seg_sum(kk * kk, bd))
    kkn = kk / jnp.maximum(nrm, 1e-12)
    k2 = k * (1.0 + (a - 1.0) * ka_ref[...])
    b = kkn * a
    bonus = _seg_sum(r * k2 * rk_ref[...], bd) * v
    return r, decay, k2, v, kkn, b, bonus, g


def _shift_rows(x, carry_ref):
    n = x.shape[0]
    prev0 = carry_ref[...]
    if n == 1:
        prev = prev0
    else:
        rolled = pltpu.roll(x, 1, 0)
        row = lax.broadcasted_iota(jnp.int32, x.shape, 0)
        prev = jnp.where(row == 0, prev0, rolled)
    carry_ref[...] = x[n - 1:n, :]
    return prev


N_SCAN_SEQS = 6


def _store_pre_outputs(o_refs, vals):
    for idx, (o_ref, val) in enumerate(zip(o_refs, vals)):
        if idx < N_SCAN_SEQS:
            for p in range(N_PAIRS):
                o_ref[0, p] = val[:, p * LANES:(p + 1) * LANES]
        else:
            o_ref[0] = val


def _pre_carry_kernel(zr_ref, zk_ref, zv_ref, zl_ref, fr_ref, fk_ref, fv_ref, fl_ref,
                      mu_ref, w0_ref, a0_ref, kk_ref, ka_ref, rk_ref, wda_ref, wg_ref, bd_ref,
                      r_o, w_o, k_o, v_o, kn_o, b_o, bo_o, g_o,
                      cr_ref, ck_ref, cv_ref, cl_ref):
    first = pl.program_id(1) == 0

    @pl.when(first)
    def _():
        cr_ref[...] = fr_ref[0]
        ck_ref[...] = fk_ref[0]
        cv_ref[...] = fv_ref[0]
        cl_ref[...] = fl_ref[0]

    zr, zk, zv, zl = zr_ref[0], zk_ref[0], zv_ref[0], zl_ref[0]
    pr = _shift_rows(zr, cr_ref)
    pk = _shift_rows(zk, ck_ref)
    pv = _shift_rows(zv, cv_ref)
    plo = _shift_rows(zl, cl_ref)
    outs = _pre_math(zr, zk, zv, zl, pr, pk, pv, plo, mu_ref, w0_ref, a0_ref, kk_ref, ka_ref,
                     rk_ref, wda_ref, wg_ref, bd_ref)
    _store_pre_outputs((r_o, w_o, k_o, v_o, kn_o, b_o, bo_o, g_o), outs)


def _pre_given_kernel(zr_ref, zk_ref, zv_ref, zl_ref, pr_ref, pk_ref, pv_ref, pl_ref,
                      mu_ref, w0_ref, a0_ref, kk_ref, ka_ref, rk_ref, wda_ref, wg_ref, bd_ref,
                      r_o, w_o, k_o, v_o, kn_o, b_o, bo_o, g_o):
    outs = _pre_math(zr_ref[0], zk_ref[0], zv_ref[0], zl_ref[0],
                     pr_ref[0], pk_ref[0], pv_ref[0], pl_ref[0],
                     mu_ref, w0_ref, a0_ref, kk_ref, ka_ref, rk_ref, wda_ref, wg_ref, bd_ref)
    _store_pre_outputs((r_o, w_o, k_o, v_o, kn_o, b_o, bo_o, g_o), outs)


def _rwkv_prelude(z3, prev3, params, tb, carry):
    bsz, t, _ = z3.shape
    nt = t // tb
    z_specs = [
        pl.BlockSpec((1, tb, WIDTH), lambda b, i: (b, i, COL_R // WIDTH)),
        pl.BlockSpec((1, tb, WIDTH), lambda b, i: (b, i, COL_K // WIDTH)),
        pl.BlockSpec((1, tb, WIDTH), lambda b, i: (b, i, COL_V // WIDTH)),
        pl.BlockSpec((1, tb, LORA_COLS), lambda b, i: (b, i, COL_L // LORA_COLS)),
    ]
    pt = 1 if carry else tb
    pidx = (lambda b, i: 0) if carry else (lambda b, i: i)
    p_specs = [
        pl.BlockSpec((1, pt, WIDTH), lambda b, i: (b, pidx(b, i), 0)),
        pl.BlockSpec((1, pt, WIDTH), lambda b, i: (b, pidx(b, i), 1)),
        pl.BlockSpec((1, pt, WIDTH), lambda b, i: (b, pidx(b, i), 2)),
        pl.BlockSpec((1, pt, LORA_COLS), lambda b, i: (b, pidx(b, i), 3 * WIDTH // LORA_COLS)),
    ]
    const = lambda shape: pl.BlockSpec(shape, lambda b, i: (0,) * len(shape))
    w_specs = [const((1, RWKV_COLS)), const((1, WIDTH)), const((1, WIDTH)), const((1, WIDTH)),
               const((1, WIDTH)), const((1, WIDTH)), const((LANES, 2 * WIDTH)),
               const((LANES, WIDTH)), const((LANES, LANES))]
    row_spec = pl.BlockSpec((1, tb, WIDTH), lambda b, i: (b, i, 0))
    pair_spec = pl.BlockSpec((1, N_PAIRS, tb, LANES), lambda b, i: (b, 0, i, 0))
    out_specs = [pair_spec] * N_SCAN_SEQS + [row_spec] * 2
    out_shape = ([jax.ShapeDtypeStruct((bsz, N_PAIRS, t, LANES), F32)] * N_SCAN_SEQS
                 + [jax.ShapeDtypeStruct((bsz, t, WIDTH), F32)] * 2)
    scratch = ([pltpu.VMEM((1, WIDTH), F32)] * 3 + [pltpu.VMEM((1, LORA_COLS), F32)]) if carry else []
    return pl.pallas_call(
        _pre_carry_kernel if carry else _pre_given_kernel,
        out_shape=out_shape,
        grid=(bsz, nt),
        in_specs=z_specs + p_specs + w_specs,
        out_specs=out_specs,
        scratch_shapes=scratch,
        compiler_params=_cparams(("parallel", "arbitrary")),
        name="rwkv_prelude",
    )(z3, z3, z3, z3, prev3, prev3, prev3, prev3, *params)


def _scan_ops(r_ref, w_ref, k_ref, v_ref, kn_ref, b_ref, mask_ref, bd_ref, bdd_ref,
              o_ref, s_ref, x_ref, bb, tt):
    maskf = mask_ref[...]
    bd = bd_ref[...]
    bdd = bdd_ref[...]

    def row(ref, bi, t):
        return jnp.concatenate(
            [jnp.broadcast_to(ref[bi, p, pl.ds(t, 1), :], (HEAD_DIM, LANES))
             for p in range(N_PAIRS)], axis=0)

    def v_diag(bi, t):
        return (maskf * row(v_ref, bi, t)).astype(BF16)

    def kk_operand(bi, t):
        return (s_ref[bi] * row(kn_ref, bi, t)).astype(BF16)

    def advance(bi, t, sa, vt):
        s_new = s_ref[bi] * row(w_ref, bi, t) - sa * row(b_ref, bi, t) + vt * row(k_ref, bi, t)
        s_ref[bi] = s_new
        return (s_new * row(r_ref, bi, t)).astype(BF16)

    def store_o(bi, t, of):
        of = maskf * of
        for p in range(N_PAIRS):
            o_ref[bi, p, pl.ds(t, 1), :] = jnp.sum(of[p * HEAD_DIM:(p + 1) * HEAD_DIM],
                                                   axis=0, keepdims=True)

    seqs = range(bb)

    def single_step():
        for bi in seqs:
            both = _dot(jnp.concatenate([kk_operand(bi, 0), v_diag(bi, 0)], axis=1), bdd)
            x = advance(bi, 0, both[:, :LANES], both[:, LANES:])
            store_o(bi, 0, _dot(x, bd))

    def reset():
        for bi in seqs:
            x_ref[bi] = jnp.zeros_like(x_ref[bi])

    def two_steps(i, carry):
        t0 = 2 * i
        vts = [_dot(jnp.concatenate([v_diag(bi, t0), v_diag(bi, t0 + 1)], axis=1), bdd)
               for bi in seqs]
        for half in range(2):
            t = t0 + half
            comb = [_dot(jnp.concatenate([kk_operand(bi, t), x_ref[bi]], axis=1), bdd)
                    for bi in seqs]
            for bi in seqs:
                x_ref[bi] = advance(bi, t, comb[bi][:, :LANES],
                                    vts[bi][:, half * LANES:(half + 1) * LANES])
            for bi in seqs:
                store_o(bi, jnp.maximum(t - 1, 0), comb[bi][:, LANES:])
        return carry

    def flush():
        for bi in seqs:
            store_o(bi, tt - 1, _dot(x_ref[bi], bd))

    return single_step, reset, two_steps, flush


def _scan_kernel(r_ref, w_ref, k_ref, v_ref, kn_ref, b_ref, s0_ref, mask_ref, bd_ref, bdd_ref,
                 o_ref, sout_ref, s_ref, x_ref, *, bb, tt):
    ti = pl.program_id(1)

    @pl.when(ti == 0)
    def _():
        s_ref[...] = s0_ref[...]

    single_step, reset, two_steps, flush = _scan_ops(
        r_ref, w_ref, k_ref, v_ref, kn_ref, b_ref, mask_ref, bd_ref, bdd_ref, o_ref, s_ref, x_ref,
        bb, tt)
    if tt == 1:
        single_step()
    else:
        reset()
        lax.fori_loop(0, tt // 2, two_steps, 0, unroll=4)
        flush()

    @pl.when(ti == pl.num_programs(1) - 1)
    def _():
        sout_ref[...] = s_ref[...]


def _rwkv_scan(seqs, s0, consts, tt):
    bsz, _, t, _ = seqs[0].shape
    bb = 2
    rows = N_PAIRS * HEAD_DIM
    seq_spec = pl.BlockSpec((bb, N_PAIRS, tt, LANES), lambda bi, ti: (bi, 0, ti, 0))
    st_spec = pl.BlockSpec((bb, rows, LANES), lambda bi, ti: (bi, 0, 0))
    const = lambda shape: pl.BlockSpec(shape, lambda bi, ti: (0,) * len(shape))
    return pl.pallas_call(
        functools.partial(_scan_kernel, bb=bb, tt=tt),
        out_shape=[jax.ShapeDtypeStruct((bsz, N_PAIRS, t, LANES), F32),
                   jax.ShapeDtypeStruct((bsz, rows, LANES), F32)],
        grid=(bsz // bb, t // tt),
        in_specs=[seq_spec] * 6 + [st_spec, const((rows, LANES)), const((LANES, LANES)),
                                   const((2 * LANES, 2 * LANES))],
        out_specs=[seq_spec, st_spec],
        scratch_shapes=[pltpu.VMEM((bb, rows, LANES), F32), pltpu.VMEM((bb, rows, LANES), BF16)],
        compiler_params=_cparams(("parallel", "arbitrary")),
        name="rwkv_scan",
    )(*seqs, s0, *consts)


def _pairs_from_heads(s):
    bsz = s.shape[0]
    s = s.reshape(bsz, N_PAIRS, 2, HEAD_DIM, HEAD_DIM).transpose(0, 1, 3, 2, 4)
    return s.reshape(bsz, N_PAIRS * HEAD_DIM, LANES)


def _heads_from_pairs(s):
    bsz = s.shape[0]
    s = s.reshape(bsz, N_PAIRS, HEAD_DIM, 2, HEAD_DIM).transpose(0, 1, 3, 2, 4)
    return s.reshape(bsz, N_HEADS, HEAD_DIM, HEAD_DIM)


def _post_kernel(o_ref, bo_ref, g_ref, lg_ref, lb_ref, bd_ref, out_ref):
    bd = bd_ref[...]
    o = jnp.concatenate([o_ref[0, p] for p in range(N_PAIRS)], axis=1)
    mean = _seg_sum(o, bd) * (1.0 / HEAD_DIM)
    d = o - mean
    var = _seg_sum(d * d, bd) * (1.0 / HEAD_DIM)
    on = d * lax.rsqrt(var + GN_EPS) * lg_ref[...] + lb_ref[...]
    out_ref[0] = ((on + bo_ref[0]) * g_ref[0]).astype(BF16)


def _rwkv_post(o, bonus, g, lnx_g, lnx_b, bd, tb):
    bsz, _, t, _ = o.shape
    spec = pl.BlockSpec((1, tb, WIDTH), lambda b, i: (b, i, 0))
    o_spec = pl.BlockSpec((1, N_PAIRS, tb, LANES), lambda b, i: (b, 0, i, 0))
    const = lambda shape: pl.BlockSpec(shape, lambda b, i: (0,) * len(shape))
    return pl.pallas_call(
        _post_kernel,
        out_shape=jax.ShapeDtypeStruct((bsz, t, WIDTH), BF16),
        grid=(bsz, t // tb),
        in_specs=[o_spec, spec, spec, const((1, WIDTH)), const((1, WIDTH)), const((LANES, LANES))],
        out_specs=spec,
        compiler_params=_cparams(("parallel", "parallel")),
        name="rwkv_post",
    )(o, bonus, g, lnx_g, lnx_b, bd)


def _fox_norm(x, gain, bd):
    ms = _seg_sum(x * x, bd) * (1.0 / HEAD_DIM)
    return x * lax.rsqrt(ms + NORM_EPS) * gain


def _log_sigmoid(x):
    return jnp.minimum(x, 0.0) - jnp.log(1.0 + jnp.exp(-jnp.abs(x)))


def _fox_prep_kernel(fq_ref, fk_ref, fv_ref, fl_ref, bf_ref, qg_ref, kg_ref, bd_ref, tril_ref,
                     pq_ref, pk_ref, kn_o, v_o, lf_o, qa_o, ka_o, va_o, c_ref):
    bd = bd_ref[...]
    q = _fox_norm(fq_ref[0], qg_ref[...], bd) * ATTN_SCALE
    kn = _fox_norm(fk_ref[0], kg_ref[...], bd)
    v = fv_ref[0]
    kn_o[0] = kn
    v_o[0] = v
    lane = lax.broadcasted_iota(jnp.int32, fl_ref[0].shape, 1)
    lf = jnp.where(lane < N_HEADS, _log_sigmoid(fl_ref[0] + bf_ref[...]), 0.0)
    lf_o[0] = lf

    @pl.when(pl.program_id(1) == 0)
    def _():
        c_ref[...] = jnp.zeros_like(c_ref)

    tril = tril_ref[...]
    h1, h2, h3 = _split3(lf)
    c = _dot(tril, h1) + _dot(tril, h2) + _dot(tril, h3) + c_ref[...]
    n = c.shape[0]
    c_ref[...] = c[n - 1:n, :]
    c1, c2, c3 = _split3(c)
    cpack = (c1.astype(F32) + pltpu.roll(c2.astype(F32), N_HEADS, 1)
             + pltpu.roll(c3.astype(F32), 2 * N_HEADS, 1)).astype(BF16)
    lane_v = lax.broadcasted_iota(jnp.int32, (n, LANES), 1)
    ones_q = jnp.where((lane_v >= HEAD_DIM + 3) & (lane_v < HEAD_DIM + 6), 1.0, 0.0)
    ones_k = jnp.where((lane_v >= HEAD_DIM) & (lane_v < HEAD_DIM + 3), 1.0, 0.0)
    for p in range(N_PAIRS):
        sl = slice(p * LANES, (p + 1) * LANES)
        lhs_q = jnp.concatenate([q[:, sl].astype(BF16), cpack], axis=1)
        lhs_k = jnp.concatenate([kn[:, sl].astype(BF16), cpack], axis=1)
        vs = v[:, sl]
        for hh in range(2):
            h = 2 * p + hh
            qa_o[0, h] = (_dot(lhs_q, pq_ref[h]) + ones_q).astype(BF16)
            ka_o[0, h] = (_dot(lhs_k, pk_ref[h]) + ones_k).astype(BF16)
            if hh == 0:
                va = jnp.where(lane_v < HEAD_DIM, vs, jnp.where(lane_v == HEAD_DIM, 1.0, 0.0))
            else:
                va = jnp.where(lane_v >= HEAD_DIM, vs, jnp.where(lane_v == 0, 1.0, 0.0))
            va_o[0, h] = va.astype(BF16)


def _fox_prep_plain_kernel(fq_ref, fk_ref, fv_ref, fl_ref, bf_ref, qg_ref, kg_ref, bd_ref,
                           kn_o, v_o, lf_o, q_o):
    bd = bd_ref[...]
    q_o[0] = _fox_norm(fq_ref[0], qg_ref[...], bd) * ATTN_SCALE
    kn_o[0] = _fox_norm(fk_ref[0], kg_ref[...], bd)
    v_o[0] = fv_ref[0]
    lane = lax.broadcasted_iota(jnp.int32, fl_ref[0].shape, 1)
    lf_o[0] = jnp.where(lane < N_HEADS, _log_sigmoid(fl_ref[0] + bf_ref[...]), 0.0)


def _fox_prep(z3, params, tb, augment):
    bsz, t, _ = z3.shape
    z_specs = [
        pl.BlockSpec((1, tb, WIDTH), lambda b, i: (b, i, COL_FQ // WIDTH)),
        pl.BlockSpec((1, tb, WIDTH), lambda b, i: (b, i, COL_FK // WIDTH)),
        pl.BlockSpec((1, tb, WIDTH), lambda b, i: (b, i, COL_FV // WIDTH)),
        pl.BlockSpec((1, tb, LANES), lambda b, i: (b, i, COL_FL // LANES)),
    ]
    const = lambda shape: pl.BlockSpec(shape, lambda b, i: (0,) * len(shape))
    w_specs = [const((1, LANES)), const((1, WIDTH)), const((1, WIDTH)), const((LANES, LANES))]
    row_spec = pl.BlockSpec((1, tb, WIDTH), lambda b, i: (b, i, 0))
    lf_spec = pl.BlockSpec((1, tb, LANES), lambda b, i: (b, i, 0))
    row_shape = jax.ShapeDtypeStruct((bsz, t, WIDTH), F32)
    lf_shape = jax.ShapeDtypeStruct((bsz, t, LANES), F32)
    if not augment:
        return pl.pallas_call(
            _fox_prep_plain_kernel,
            out_shape=[row_shape, row_shape, lf_shape, row_shape],
            grid=(bsz, t // tb),
            in_specs=z_specs + w_specs,
            out_specs=[row_spec, row_spec, lf_spec, row_spec],
            compiler_params=_cparams(("parallel", "arbitrary")),
            name="fox_prep_sample",
        )(z3, z3, z3, z3, *params[:4])
    aug_spec = pl.BlockSpec((1, N_HEADS, tb, LANES), lambda b, i: (b, 0, i, 0))
    aug_shape = jax.ShapeDtypeStruct((bsz, N_HEADS, t, LANES), BF16)
    w_specs += [const((tb, tb)), const((N_HEADS, 2 * LANES, LANES)),
                const((N_HEADS, 2 * LANES, LANES))]
    return pl.pallas_call(
        _fox_prep_kernel,
        out_shape=[row_shape, row_shape, lf_shape, aug_shape, aug_shape, aug_shape],
        grid=(bsz, t // tb),
        in_specs=z_specs + w_specs,
        out_specs=[row_spec, row_spec, lf_spec, aug_spec, aug_spec, aug_spec],
        scratch_shapes=[pltpu.VMEM((1, LANES), F32)],
        compiler_params=_cparams(("parallel", "arbitrary")),
        name="fox_prep",
    )(z3, z3, z3, z3, *params)


def _place_matrices():
    h = jnp.arange(N_HEADS)[:, None, None]
    row = jnp.arange(2 * LANES)[None, :, None]
    col = jnp.arange(LANES)[None, None, :]
    data = (row < LANES) & (row == (h % 2) * HEAD_DIM + col) & (col < HEAD_DIM)
    piece = (row - LANES) // N_HEADS
    is_c = (row >= LANES) & (row < LANES + 3 * N_HEADS) & ((row - LANES) % N_HEADS == h)
    cq = is_c & (col == HEAD_DIM + piece)
    ck = is_c & (col == HEAD_DIM + 3 + piece)
    pq = (data | cq).astype(F32)
    pk = data.astype(F32) - ck.astype(F32)
    return pq.astype(BF16), pk.astype(BF16)


def _flash_kernel(q_ref, k_ref, v_ref, og_ref, o_ref, *, tq, tk, nh):
    qi = pl.program_id(2)
    n_full = (qi * tq) // tk
    row = lax.broadcasted_iota(jnp.int32, (tq, tk), 0) + qi * tq
    col = lax.broadcasted_iota(jnp.int32, (tq, tk), 1) + n_full * tk
    causal = col <= row
    qs = [q_ref[0, hh] for hh in range(nh)]

    def block(j, carry, masked):
        start = pl.multiple_of(j * tk, tk)
        new = []
        for hh in range(nh):
            m, acc = carry[2 * hh], carry[2 * hh + 1]
            kb = k_ref[0, hh, pl.ds(start, tk), :]
            vb = v_ref[0, hh, pl.ds(start, tk), :]
            s = lax.dot_general(qs[hh], kb, (((1,), (1,)), ((), ())), preferred_element_type=F32)
            if masked:
                s = jnp.where(causal, s, NEG_BIG)
            m_new = jnp.maximum(m, jnp.max(s, axis=1, keepdims=True))
            p = jnp.exp(s - m_new)
            new += [m_new, jnp.exp(m - m_new) * acc + _dot(p.astype(BF16), vb)]
        return tuple(new)

    init = (jnp.full((tq, 1), NEG_BIG, F32), jnp.zeros((tq, LANES), F32)) * nh
    carry = lax.fori_loop(0, n_full, lambda j, c: block(j, c, False), init)
    carry = block(n_full, carry, True)
    lane = lax.broadcasted_iota(jnp.int32, (tq, LANES), 1)
    for pp in range(nh // 2):
        acc_e, acc_o = carry[4 * pp + 1], carry[4 * pp + 3]
        o = jnp.where(lane < HEAD_DIM, acc_e / acc_e[:, HEAD_DIM:HEAD_DIM + 1],
                      acc_o / acc_o[:, 0:1])
        sl = slice(pp * LANES, (pp + 1) * LANES)
        o_ref[0, :, sl] = (o * _sigmoid(og_ref[0, :, sl])).astype(BF16)


def _fox_flash(qa, ka, va, z3, tq, tk, nh):
    bsz, _, t, _ = qa.shape
    gw = (nh // 2) * LANES
    return pl.pallas_call(
        functools.partial(_flash_kernel, tq=tq, tk=tk, nh=nh),
        out_shape=jax.ShapeDtypeStruct((bsz, t, WIDTH), BF16),
        grid=(bsz, N_HEADS // nh, t // tq),
        in_specs=[
            pl.BlockSpec((1, nh, tq, LANES), lambda b, g, i: (b, g, i, 0)),
            pl.BlockSpec((1, nh, t, LANES), lambda b, g, i: (b, g, 0, 0)),
            pl.BlockSpec((1, nh, t, LANES), lambda b, g, i: (b, g, 0, 0)),
            pl.BlockSpec((1, tq, gw), lambda b, g, i: (b, i, COL_OG // gw + g)),
        ],
        out_specs=pl.BlockSpec((1, tq, gw), lambda b, g, i: (b, i, g)),
        compiler_params=_cparams(("parallel", "parallel", "arbitrary")),
        name="fox_flash",
    )(qa, ka, va, z3)


def _decode_ops(qt_ref, kc_ref, vc_ref, lc_ref, og_ref, lt_ref, on_ref, o_ref,
                m_ref, l_ref, cy_ref, acc_ref):
    later = lt_ref[...]
    ones = on_ref[...]

    def rows(fn):
        return jnp.concatenate([fn(h) for h in range(N_HEADS)], axis=0)

    def init():
        m_ref[...] = rows(lambda h: jnp.sum(qt_ref[0, h] * kc_ref[0, h], axis=0, keepdims=True))
        l_ref[...] = jnp.ones_like(l_ref)
        cy_ref[...] = lc_ref[0]
        lane = lax.broadcasted_iota(jnp.int32, (HEAD_DIM, LANES), 1)
        for h in range(N_HEADS):
            acc_ref[h] = jnp.where(lane == 0, vc_ref[0, h], 0.0)

    def page(lf, k_tile, v_tile):
        h1, h2, h3 = _split3(lf)
        bias = _dot(h1, later) + _dot(h2, later) + _dot(h3, later) + cy_ref[...]
        cy_ref[...] = cy_ref[...] + _dot(h1, ones) + _dot(h2, ones) + _dot(h3, ones)
        s = rows(lambda h: jnp.sum(qt_ref[0, h] * k_tile(h), axis=0, keepdims=True)) + bias
        m_old = m_ref[...]
        m_new = jnp.maximum(m_old, jnp.max(s, axis=1, keepdims=True))
        alpha = jnp.exp(m_old - m_new)
        p = jnp.exp(s - m_new)
        m_ref[...] = m_new
        l_ref[...] = alpha * l_ref[...] + jnp.sum(p, axis=1, keepdims=True)
        for h in range(N_HEADS):
            acc_ref[h] = alpha[h:h + 1, :] * acc_ref[h] + p[h:h + 1, :] * v_tile(h)

    def finish():
        ones8 = ones[0:8, :]

        def token_sum(h):
            nt = lambda a: lax.dot_general(ones8, a, (((1,), (1,)), ((), ())),
                                           preferred_element_type=F32)
            a1, a2, a3 = _split3(acc_ref[h])
            return (nt(a1) + nt(a2) + nt(a3))[0:1, :]

        o_ref[0] = (rows(token_sum) / l_ref[:, 0:HEAD_DIM]) * _sigmoid(og_ref[0])

    return init, page, finish


def _scan_decode_kernel(pt_ref, r_ref, w_ref, k_ref, v_ref, kn_ref, b_ref, s0_ref, mask_ref,
                        bd_ref, bdd_ref, qt_ref, kc_ref, vc_ref, lc_ref, og_ref, lt_ref, on_ref,
                        ck_hbm, cv_hbm, cl_hbm, o_ref, sout_ref, od_ref,
                        s_ref, x_ref, kbuf, vbuf, lbuf, sem, m_ref, l_ref, cy_ref, acc_ref,
                        *, bb, tt, npg, n_pages):
    g = pl.program_id(0)

    @pl.when(g == 0)
    def _():
        s_ref[...] = s0_ref[...]

    _, reset, two_steps, flush = _scan_ops(
        r_ref, w_ref, k_ref, v_ref, kn_ref, b_ref, mask_ref, bd_ref, bdd_ref, o_ref, s_ref, x_ref,
        bb, tt)
    d_init, d_page, d_finish = _decode_ops(qt_ref, kc_ref, vc_ref, lc_ref, og_ref, lt_ref, on_ref,
                                           od_ref, m_ref, l_ref, cy_ref, acc_ref)
    n_chunks = n_pages // npg
    pairs_per_chunk = tt // (2 * n_chunks)

    def chunk_copies(seq, c, slot):
        cps = []
        for i in range(npg):
            pg = pt_ref[seq, n_pages - 1 - (c * npg + i)]
            cps += [pltpu.make_async_copy(ck_hbm.at[pg], kbuf.at[slot, i], sem.at[slot, 0]),
                    pltpu.make_async_copy(cv_hbm.at[pg], vbuf.at[slot, i], sem.at[slot, 1]),
                    pltpu.make_async_copy(cl_hbm.at[pg], lbuf.at[slot, i], sem.at[slot, 2])]
        return cps

    @pl.when(g == 0)
    def _():
        for cp in chunk_copies(0, 0, 0):
            cp.start()

    reset()
    d_init()

    def chunk(c, carry):
        slot = lax.rem(c, 2)
        for cp in chunk_copies(g, c, slot):
            cp.wait()

        @pl.when(c + 1 < n_chunks)
        def _():
            for cp in chunk_copies(g, c + 1, 1 - slot):
                cp.start()

        @pl.when((c + 1 == n_chunks) & (g + 1 < pl.num_programs(0)))
        def _():
            for cp in chunk_copies(g + 1, 0, 0):
                cp.start()

        for i in range(npg):
            d_page(lbuf[slot, i], lambda h, i=i: kbuf[slot, i, h], lambda h, i=i: vbuf[slot, i, h])
        lax.fori_loop(0, pairs_per_chunk, lambda q, cr: two_steps(c * pairs_per_chunk + q, cr), 0,
                      unroll=min(pairs_per_chunk, 4))
        return carry

    lax.fori_loop(0, n_chunks, chunk, 0)
    flush()
    d_finish()

    @pl.when(g == pl.num_programs(0) - 1)
    def _():
        sout_ref[...] = s_ref[...]


def _rwkv_scan_fox_decode(scan_seqs, s0, scan_consts, page_table, qt, kc, vc, lfc, og,
                          cache_kt, cache_vt, cache_lt, later, ones, npg):
    bsz, _, t, _ = scan_seqs[0].shape
    dbsz, n_pages = page_table.shape
    page = cache_kt.shape[-1]
    rows = N_PAIRS * HEAD_DIM
    tt = t // dbsz
    n_chunks = n_pages // npg
    assert tt * dbsz == t and n_chunks * npg == n_pages and n_chunks % 2 == 0
    assert tt % (2 * n_chunks) == 0
    seq_spec = pl.BlockSpec((bsz, N_PAIRS, tt, LANES), lambda g, pt: (0, 0, g, 0))
    st_spec = pl.BlockSpec((bsz, rows, LANES), lambda g, pt: (0, 0, 0))
    const = lambda shape: pl.BlockSpec(shape, lambda g, pt: (0,) * len(shape))
    tile_spec = pl.BlockSpec((1, N_HEADS, HEAD_DIM, page), lambda g, pt: (g, 0, 0, 0))
    hbm = pl.BlockSpec(memory_space=pl.ANY)
    in_specs = ([seq_spec] * N_SCAN_SEQS
                + [st_spec, const((rows, LANES)), const((LANES, LANES)), const((2 * LANES, 2 * LANES))]
                + [tile_spec] * 3
                + [pl.BlockSpec((1, N_HEADS, page), lambda g, pt: (g, 0, 0)),
                   pl.BlockSpec((1, N_HEADS, HEAD_DIM), lambda g, pt: (g, 0, 0)),
                   const((page, page)), const((page, page)), hbm, hbm, hbm])
    stat = pltpu.VMEM((N_HEADS, page), F32)
    grid_spec = pltpu.PrefetchScalarGridSpec(
        num_scalar_prefetch=1,
        grid=(dbsz,),
        in_specs=in_specs,
        out_specs=[seq_spec, st_spec,
                   pl.BlockSpec((1, N_HEADS, HEAD_DIM), lambda g, pt: (g, 0, 0))],
        scratch_shapes=[pltpu.VMEM((bsz, rows, LANES), F32), pltpu.VMEM((bsz, rows, LANES), BF16),
                        pltpu.VMEM((2, npg, N_HEADS, HEAD_DIM, page), F32),
                        pltpu.VMEM((2, npg, N_HEADS, HEAD_DIM, page), F32),
                        pltpu.VMEM((2, npg, N_HEADS, page), F32),
                        pltpu.SemaphoreType.DMA((2, 3)),
                        stat, stat, stat, pltpu.VMEM((N_HEADS, HEAD_DIM, page), F32)],
    )
    return pl.pallas_call(
        functools.partial(_scan_decode_kernel, bb=bsz, tt=tt, npg=npg, n_pages=n_pages),
        out_shape=[jax.ShapeDtypeStruct((bsz, N_PAIRS, t, LANES), F32),
                   jax.ShapeDtypeStruct((bsz, rows, LANES), F32),
                   jax.ShapeDtypeStruct((dbsz, N_HEADS, HEAD_DIM), F32)],
        grid_spec=grid_spec,
        compiler_params=_cparams(("arbitrary",)),
        name="rwkv_scan_fox_decode",
    )(page_table, *scan_seqs, s0, *scan_consts, qt, kc, vc, lfc, og, later, ones,
      cache_kt, cache_vt, cache_lt)


def _out_proj_kernel(x_ref, oa_ref, ob_ref, wa_ref, wb_ref, h_ref):
    h_ref[...] = x_ref[...] + _dot(oa_ref[...], wa_ref[...]) + _dot(ob_ref[...], wb_ref[...])


def _out_proj(x, oa, ob, w_out, tm, tn):
    m, d = x.shape
    return pl.pallas_call(
        _out_proj_kernel,
        out_shape=jax.ShapeDtypeStruct((m, d), F32),
        grid=(m // tm, d // tn),
        in_specs=[
            pl.BlockSpec((tm, tn), lambda i, j: (i, j)),
            pl.BlockSpec((tm, WIDTH), lambda i, j: (i, 0)),
            pl.BlockSpec((tm, WIDTH), lambda i, j: (i, 0)),
            pl.BlockSpec((WIDTH, tn), lambda i, j: (0, j)),
            pl.BlockSpec((WIDTH, tn), lambda i, j: (1, j)),
        ],
        out_specs=pl.BlockSpec((tm, tn), lambda i, j: (i, j)),
        compiler_params=_cparams(("parallel", "arbitrary")),
        name="out_proj",
    )(x, oa, ob, w_out, w_out)


def _ffn_kernel(h_ref, gn_ref, wg_ref, wu_ref, wd_ref, gf_ref, y_ref, hn_ref, acc_ref):
    f = pl.program_id(1)

    @pl.when(f == 0)
    def _():
        h = h_ref[...]
        ms = jnp.mean(h * h, axis=-1, keepdims=True)
        hn_ref[...] = (h * lax.rsqrt(ms + NORM_EPS) * gn_ref[...]).astype(BF16)
        acc_ref[...] = jnp.zeros_like(acc_ref)

    hn = hn_ref[...]
    gate = _dot(hn, wg_ref[...])
    up = _dot(hn, wu_ref[...])
    act = (gate * _sigmoid(gate) * up).astype(BF16)
    acc_ref[...] += _dot(act, wd_ref[...])

    @pl.when(f == pl.num_programs(1) - 1)
    def _():
        hh = h_ref[...] + acc_ref[...]
        ms = jnp.mean(hh * hh, axis=-1, keepdims=True)
        y_ref[...] = hh * lax.rsqrt(ms + NORM_EPS) * gf_ref[...]


def _ffn(h, g_ffn, wg, wu, wd, g_final, tm, tf):
    m, d = h.shape
    dff = wg.shape[1]
    return pl.pallas_call(
        _ffn_kernel,
        out_shape=jax.ShapeDtypeStruct((m, d), F32),
        grid=(m // tm, dff // tf),
        in_specs=[
            pl.BlockSpec((tm, d), lambda i, f: (i, 0)),
            pl.BlockSpec((1, d), lambda i, f: (0, 0)),
            pl.BlockSpec((d, tf), lambda i, f: (0, f)),
            pl.BlockSpec((d, tf), lambda i, f: (0, f)),
            pl.BlockSpec((tf, d), lambda i, f: (f, 0)),
            pl.BlockSpec((1, d), lambda i, f: (0, 0)),
        ],
        out_specs=pl.BlockSpec((tm, d), lambda i, f: (i, 0)),
        scratch_shapes=[pltpu.VMEM((tm, d), BF16), pltpu.VMEM((tm, d), F32)],
        compiler_params=_cparams(("parallel", "arbitrary")),
        name="ffn",
    )(h, g_ffn, wg, wu, wd, g_final)


def _block_diag_ones():
    i = jnp.arange(LANES)
    return (i[:, None] // HEAD_DIM == i[None, :] // HEAD_DIM).astype(BF16)


def _pick(n, cap):
    t = min(n, cap)
    while n % t:
        t //= 2
    return t


def kernel(x_prompt, x_sample, cache_k, cache_v, cache_logf, state_wkv, state_shift, page_table,
           norm_mix, w_in, mu_shift, w0, w_decay_up, a0, w_aaa_up, w_gate_up, k_k, k_a, r_k,
           lnx_g, lnx_b, b_f, q_norm, k_norm, w_out, norm_ffn, w_ffn_gate, w_ffn_up, w_ffn_down,
           norm_final):
    bsz, seq, d = x_prompt.shape
    dbsz = x_sample.shape[0]
    lyr = 0
    rw = RWKV_COLS

    wi = w_in[lyr]
    w_z = jnp.concatenate(
        [wi[:, rw:rw + 4 * WIDTH], wi[:, :rw], wi[:, rw + 4 * WIDTH:],
         jnp.zeros((d, Z_COLS - N_Z_USED), F32)], axis=1).astype(BF16)
    g_mix = norm_mix[lyr][None, :]
    wda = jnp.zeros((LANES, 2 * WIDTH), F32)
    wda = wda.at[:HEAD_DIM, :WIDTH].set(w_decay_up[lyr]).at[HEAD_DIM:, WIDTH:].set(w_aaa_up[lyr])
    bd = _block_diag_ones()
    pre_params = (mu_shift[lyr][None, :], w0[lyr][None, :], a0[lyr][None, :], k_k[lyr][None, :],
                  k_a[lyr][None, :], r_k[lyr].reshape(1, WIDTH), wda.astype(BF16),
                  w_gate_up[lyr].astype(BF16), bd)
    li = jnp.arange(LANES)
    maskf = (li[None, :] % HEAD_DIM
             == jnp.arange(N_PAIRS * HEAD_DIM)[:, None] % HEAD_DIM).astype(F32)
    zb = jnp.zeros_like(bd)
    bdd = jnp.concatenate([jnp.concatenate([bd, zb], axis=1),
                           jnp.concatenate([zb, bd], axis=1)], axis=0)
    scan_consts = (maskf, bd, bdd)
    lg, lb = lnx_g[lyr][None, :], lnx_b[lyr][None, :]
    bfp = jnp.zeros((1, LANES), F32).at[0, :N_HEADS].set(b_f[lyr])
    qg = jnp.tile(q_norm[lyr], N_HEADS)[None, :]
    kg = jnp.tile(k_norm[lyr], N_HEADS)[None, :]
    tb_fox = _pick(seq, 256)
    tril = (jnp.arange(tb_fox)[:, None] >= jnp.arange(tb_fox)[None, :]).astype(BF16)
    pq, pk = _place_matrices()
    w_o = w_out[lyr].astype(BF16)
    wg = w_ffn_gate[lyr].astype(BF16)
    wu = w_ffn_up[lyr].astype(BF16)
    wd = w_ffn_down[lyr].astype(BF16)
    g_ffn = norm_ffn[lyr][None, :]
    g_fin = norm_final[None, :]

    m_p = bsz * seq
    xp = x_prompt.reshape(m_p, d)
    z_p = _norm_matmul(xp, g_mix, w_z, _pick(m_p, 1024), 1536)
    z3p = z_p.reshape(bsz, seq, Z_COLS)
    first = jnp.zeros((bsz, 1, rw), F32)
    pre_p = _rwkv_prelude(z3p, first, pre_params, _pick(seq, 256), True)
    bonus_p, gate_p = pre_p[N_SCAN_SEQS:]

    xs = x_sample.reshape(dbsz, d)
    z_s = _norm_matmul(xs, g_mix, w_z, dbsz, 512)
    z3s = z_s.reshape(1, dbsz, Z_COLS)
    prev_s = state_shift[lyr].reshape(1, dbsz, rw)
    pre_s = _rwkv_prelude(z3s, prev_s, pre_params, dbsz, False)
    bonus_s, gate_s = pre_s[N_SCAN_SEQS:]
    fkn_s, fv_s, lf_s, q_s = _fox_prep(z3s, (bfp, qg, kg, bd), dbsz, False)

    page = cache_k.shape[2]
    lane_tile = lambda a: jnp.broadcast_to(a.reshape(dbsz, N_HEADS, HEAD_DIM, 1),
                                           (dbsz, N_HEADS, HEAD_DIM, page))
    lfc = jnp.broadcast_to(lf_s[0, :, :N_HEADS, None], (dbsz, N_HEADS, page))
    og_s = z_s[:, COL_OG:COL_OG + WIDTH].reshape(dbsz, N_HEADS, HEAD_DIM)
    later = (jnp.arange(page)[:, None] > jnp.arange(page)[None, :]).astype(BF16)
    s0 = jnp.zeros((bsz, N_PAIRS * HEAD_DIM, LANES), F32)
    o_p, s_p, ob_s = _rwkv_scan_fox_decode(
        pre_p[:N_SCAN_SEQS], s0, scan_consts, page_table, lane_tile(q_s), lane_tile(fkn_s),
        lane_tile(fv_s), lfc, og_s, cache_k[lyr].transpose(0, 2, 3, 1),
        cache_v[lyr].transpose(0, 2, 3, 1), cache_logf[lyr].transpose(0, 2, 1), later,
        jnp.ones((page, page), BF16), _pick(page_table.shape[1], 8))

    oa_p = _rwkv_post(o_p, bonus_p, gate_p, lg, lb, bd, _pick(seq, 512))
    fkn_p, fv_p, lf_p, qa, ka, va = _fox_prep(z3p, (bfp, qg, kg, bd, tril, pq, pk), tb_fox, True)
    ob_p = _fox_flash(qa, ka, va, z3p, _pick(seq, 1024), _pick(seq, 1024), 2)
    h_p = _out_proj(xp, oa_p.reshape(m_p, WIDTH), ob_p.reshape(m_p, WIDTH), w_o,
                    _pick(m_p, 1024), 512)
    y_p = _ffn(h_p, g_ffn, wg, wu, wd, g_fin, _pick(m_p, 512), 512)

    seq_major = lambda a: a.transpose(2, 1, 0, 3)
    o_s, s_s = _rwkv_scan([seq_major(a) for a in pre_s[:N_SCAN_SEQS]],
                          _pairs_from_heads(state_wkv[lyr]), scan_consts, 1)
    oa_s = _rwkv_post(seq_major(o_s), bonus_s, gate_s, lg, lb, bd, dbsz)
    h_s = _out_proj(xs, oa_s.reshape(dbsz, WIDTH), ob_s.reshape(dbsz, WIDTH).astype(BF16),
                    w_o, dbsz, 512)
    y_s = _ffn(h_s, g_ffn, wg, wu, wd, g_fin, dbsz, 512)

    hd = (N_HEADS, HEAD_DIM)
    return (
        y_p.reshape(bsz, seq, d),
        y_s.reshape(dbsz, 1, d),
        fkn_p.reshape(1, bsz, seq, *hd),
        fv_p.reshape(1, bsz, seq, *hd),
        lf_p[:, :, :N_HEADS].reshape(1, bsz, seq, N_HEADS),
        _heads_from_pairs(s_p)[None],
        z3p[:, seq - 1, COL_R:COL_R + rw][None],
        fkn_s.reshape(1, dbsz, 1, *hd),
        fv_s.reshape(1, dbsz, 1, *hd),
        lf_s[0, :, :N_HEADS].reshape(1, dbsz, 1, N_HEADS),
        _heads_from_pairs(s_s)[None],
        z_s[:, COL_R:COL_R + rw][None],
    )
```

```python
import functools

import jax
import jax.numpy as jnp
from jax import lax
from jax.experimental import pallas as pl
from jax.experimental.pallas import tpu as pltpu

F32 = jnp.float32
BF16 = jnp.bfloat16

HEAD_DIM = 64
N_HEADS = 16
WIDTH = N_HEADS * HEAD_DIM
N_PAIRS = N_HEADS // 2
LANES = 128
LORA_COLS = 256
RWKV_COLS = 3 * WIDTH + LORA_COLS
NORM_EPS = 1e-6
GN_EPS = 64e-5
ATTN_SCALE = HEAD_DIM ** -0.5
EXP_NEG_HALF = 0.6065306597126334
NEG_BIG = -1e30

Z_COLS = 7680
COL_FQ, COL_FK, COL_FV, COL_OG = 0, 1024, 2048, 3072
COL_R, COL_K, COL_V, COL_L = 4096, 5120, 6144, 7168
COL_FL = 7424
N_Z_USED = 7440

VMEM_LIMIT = 56 * 1024 * 1024


def _cparams(sem):
    return pltpu.CompilerParams(dimension_semantics=sem, vmem_limit_bytes=VMEM_LIMIT)


def _sigmoid(x):
    return 1.0 / (1.0 + jnp.exp(-x))


def _dot(a, b):
    return jnp.dot(a, b, preferred_element_type=F32)


def _split2(x):
    hi = x.astype(BF16)
    lo = (x - hi.astype(F32)).astype(BF16)
    return hi, lo


def _split3(x):
    h1 = x.astype(BF16)
    r1 = x - h1.astype(F32)
    h2 = r1.astype(BF16)
    h3 = (r1 - h2.astype(F32)).astype(BF16)
    return h1, h2, h3


def _seg_sum(x, bd):
    parts = []
    for c in range(x.shape[1] // LANES):
        hi, lo = _split2(x[:, c * LANES:(c + 1) * LANES])
        parts.append(_dot(hi, bd) + _dot(lo, bd))
    return parts[0] if len(parts) == 1 else jnp.concatenate(parts, axis=1)


def _norm_matmul_kernel(x_ref, g_ref, w_ref, o_ref, xn_ref):
    @pl.when(pl.program_id(1) == 0)
    def _():
        x = x_ref[...]
        ms = jnp.mean(x * x, axis=-1, keepdims=True)
        xn_ref[...] = (x * lax.rsqrt(ms + NORM_EPS) * g_ref[...]).astype(BF16)

    o_ref[...] = _dot(xn_ref[...], w_ref[...])


def _norm_matmul(x, g, w, tm, tn):
    m, d = x.shape
    n = w.shape[1]
    return pl.pallas_call(
        _norm_matmul_kernel,
        out_shape=jax.ShapeDtypeStruct((m, n), F32),
        grid=(m // tm, n // tn),
        in_specs=[
            pl.BlockSpec((tm, d), lambda i, j: (i, 0)),
            pl.BlockSpec((1, d), lambda i, j: (0, 0)),
            pl.BlockSpec((d, tn), lambda i, j: (0, j)),
        ],
        out_specs=pl.BlockSpec((tm, tn), lambda i, j: (i, j)),
        scratch_shapes=[pltpu.VMEM((tm, d), BF16)],
        compiler_params=_cparams(("parallel", "arbitrary")),
        name="in_proj",
    )(x, g, w)


def _pre_math(zr, zk, zv, zl, pr, pk, pv, plo, mu_ref, w0_ref, a0_ref, kk_ref, ka_ref, rk_ref,
              wda_ref, wg_ref, bd_ref):
    bd = bd_ref[...]
    r = zr + (pr - zr) * mu_ref[:, 0:WIDTH]
    k = zk + (pk - zk) * mu_ref[:, WIDTH:2 * WIDTH]
    v = zv + (pv - zv) * mu_ref[:, 2 * WIDTH:3 * WIDTH]
    lo = zl + (plo - zl) * mu_ref[:, 3 * WIDTH:3 * WIDTH + LORA_COLS]
    l0 = lo[:, 0:LANES]
    lane = lax.broadcasted_iota(jnp.int32, l0.shape, 1)
    lhs = jnp.where(lane < HEAD_DIM, jnp.tanh(l0), l0).astype(BF16)
    da = _dot(lhs, wda_ref[...])
    decay = jnp.exp(-EXP_NEG_HALF * _sigmoid(w0_ref[...] + da[:, 0:WIDTH]))
    a = _sigmoid(a0_ref[...] + da[:, WIDTH:2 * WIDTH])
    g = _dot(_sigmoid(lo[:, LANES:2 * LANES]).astype(BF16), wg_ref[...])
    kk = k * kk_ref[...]
    nrm = jnp.sqrt(_seg_sum(kk * kk, bd))
    kkn = kk / jnp.maximum(nrm, 1e-12)
    k2 = k * (1.0 + (a - 1.0) * ka_ref[...])
    b = kkn * a
    bonus = _seg_sum(r * k2 * rk_ref[...], bd) * v
    return r, decay, k2, v, kkn, b, bonus, g


def _shift_rows(x, carry_ref):
    n = x.shape[0]
    prev0 = carry_ref[...]
    if n == 1:
        prev = prev0
    else:
        rolled = pltpu.roll(x, 1, 0)
        row = lax.broadcasted_iota(jnp.int32, x.shape, 0)
        prev = jnp.where(row == 0, prev0, rolled)
    carry_ref[...] = x[n - 1:n, :]
    return prev


N_SCAN_SEQS = 6


def _store_pre_outputs(o_refs, vals):
    for idx, (o_ref, val) in enumerate(zip(o_refs, vals)):
        if idx < N_SCAN_SEQS:
            for p in range(N_PAIRS):
                o_ref[0, p] = val[:, p * LANES:(p + 1) * LANES]
        else:
            o_ref[0] = val


def _pre_carry_kernel(zr_ref, zk_ref, zv_ref, zl_ref, fr_ref, fk_ref, fv_ref, fl_ref,
                      mu_ref, w0_ref, a0_ref, kk_ref, ka_ref, rk_ref, wda_ref, wg_ref, bd_ref,
                      r_o, w_o, k_o, v_o, kn_o, b_o, bo_o, g_o,
                      cr_ref, ck_ref, cv_ref, cl_ref):
    first = pl.program_id(1) == 0

    @pl.when(first)
    def _():
        cr_ref[...] = fr_ref[0]
        ck_ref[...] = fk_ref[0]
        cv_ref[...] = fv_ref[0]
        cl_ref[...] = fl_ref[0]

    zr, zk, zv, zl = zr_ref[0], zk_ref[0], zv_ref[0], zl_ref[0]
    pr = _shift_rows(zr, cr_ref)
    pk = _shift_rows(zk, ck_ref)
    pv = _shift_rows(zv, cv_ref)
    plo = _shift_rows(zl, cl_ref)
    outs = _pre_math(zr, zk, zv, zl, pr, pk, pv, plo, mu_ref, w0_ref, a0_ref, kk_ref, ka_ref,
                     rk_ref, wda_ref, wg_ref, bd_ref)
    _store_pre_outputs((r_o, w_o, k_o, v_o, kn_o, b_o, bo_o, g_o), outs)


def _pre_given_kernel(zr_ref, zk_ref, zv_ref, zl_ref, pr_ref, pk_ref, pv_ref, pl_ref,
                      mu_ref, w0_ref, a0_ref, kk_ref, ka_ref, rk_ref, wda_ref, wg_ref, bd_ref,
                      r_o, w_o, k_o, v_o, kn_o, b_o, bo_o, g_o):
    outs = _pre_math(zr_ref[0], zk_ref[0], zv_ref[0], zl_ref[0],
                     pr_ref[0], pk_ref[0], pv_ref[0], pl_ref[0],
                     mu_ref, w0_ref, a0_ref, kk_ref, ka_ref, rk_ref, wda_ref, wg_ref, bd_ref)
    _store_pre_outputs((r_o, w_o, k_o, v_o, kn_o, b_o, bo_o, g_o), outs)


def _rwkv_prelude(z3, prev3, params, tb, carry):
    bsz, t, _ = z3.shape
    nt = t // tb
    z_specs = [
        pl.BlockSpec((1, tb, WIDTH), lambda b, i: (b, i, COL_R // WIDTH)),
        pl.BlockSpec((1, tb, WIDTH), lambda b, i: (b, i, COL_K // WIDTH)),
        pl.BlockSpec((1, tb, WIDTH), lambda b, i: (b, i, COL_V // WIDTH)),
        pl.BlockSpec((1, tb, LORA_COLS), lambda b, i: (b, i, COL_L // LORA_COLS)),
    ]
    pt = 1 if carry else tb
    pidx = (lambda b, i: 0) if carry else (lambda b, i: i)
    p_specs = [
        pl.BlockSpec((1, pt, WIDTH), lambda b, i: (b, pidx(b, i), 0)),
        pl.BlockSpec((1, pt, WIDTH), lambda b, i: (b, pidx(b, i), 1)),
        pl.BlockSpec((1, pt, WIDTH), lambda b, i: (b, pidx(b, i), 2)),
        pl.BlockSpec((1, pt, LORA_COLS), lambda b, i: (b, pidx(b, i), 3 * WIDTH // LORA_COLS)),
    ]
    const = lambda shape: pl.BlockSpec(shape, lambda b, i: (0,) * len(shape))
    w_specs = [const((1, RWKV_COLS)), const((1, WIDTH)), const((1, WIDTH)), const((1, WIDTH)),
               const((1, WIDTH)), const((1, WIDTH)), const((LANES, 2 * WIDTH)),
               const((LANES, WIDTH)), const((LANES, LANES))]
    row_spec = pl.BlockSpec((1, tb, WIDTH), lambda b, i: (b, i, 0))
    pair_spec = pl.BlockSpec((1, N_PAIRS, tb, LANES), lambda b, i: (b, 0, i, 0))
    out_specs = [pair_spec] * N_SCAN_SEQS + [row_spec] * 2
    out_shape = ([jax.ShapeDtypeStruct((bsz, N_PAIRS, t, LANES), F32)] * N_SCAN_SEQS
                 + [jax.ShapeDtypeStruct((bsz, t, WIDTH), F32)] * 2)
    scratch = ([pltpu.VMEM((1, WIDTH), F32)] * 3 + [pltpu.VMEM((1, LORA_COLS), F32)]) if carry else []
    return pl.pallas_call(
        _pre_carry_kernel if carry else _pre_given_kernel,
        out_shape=out_shape,
        grid=(bsz, nt),
        in_specs=z_specs + p_specs + w_specs,
        out_specs=out_specs,
        scratch_shapes=scratch,
        compiler_params=_cparams(("parallel", "arbitrary")),
        name="rwkv_prelude",
    )(z3, z3, z3, z3, prev3, prev3, prev3, prev3, *params)


def _scan_ops(r_ref, w_ref, k_ref, v_ref, kn_ref, b_ref, mask_ref, bd_ref, bdd_ref,
              o_ref, s_ref, x_ref, bb, tt):
    maskf = mask_ref[...]
    bd = bd_ref[...]
    bdd = bdd_ref[...]

    def row(ref, bi, t):
        return jnp.concatenate(
            [jnp.broadcast_to(ref[bi, p, pl.ds(t, 1), :], (HEAD_DIM, LANES))
             for p in range(N_PAIRS)], axis=0)

    def v_diag(bi, t):
        return (maskf * row(v_ref, bi, t)).astype(BF16)

    def kk_operand(bi, t):
        return (s_ref[bi] * row(kn_ref, bi, t)).astype(BF16)

    def advance(bi, t, sa, vt):
        s_new = s_ref[bi] * row(w_ref, bi, t) - sa * row(b_ref, bi, t) + vt * row(k_ref, bi, t)
        s_ref[bi] = s_new
        return (s_new * row(r_ref, bi, t)).astype(BF16)

    def store_o(bi, t, of):
        of = maskf * of
        for p in range(N_PAIRS):
            o_ref[bi, p, pl.ds(t, 1), :] = jnp.sum(of[p * HEAD_DIM:(p + 1) * HEAD_DIM],
                                                   axis=0, keepdims=True)

    seqs = range(bb)

    def single_step():
        for bi in seqs:
            both = _dot(jnp.concatenate([kk_operand(bi, 0), v_diag(bi, 0)], axis=1), bdd)
            x = advance(bi, 0, both[:, :LANES], both[:, LANES:])
            store_o(bi, 0, _dot(x, bd))

    def reset():
        for bi in seqs:
            x_ref[bi] = jnp.zeros_like(x_ref[bi])

    def two_steps(i, carry):
        t0 = 2 * i
        vts = [_dot(jnp.concatenate([v_diag(bi, t0), v_diag(bi, t0 + 1)], axis=1), bdd)
               for bi in seqs]
        for half in range(2):
            t = t0 + half
            comb = [_dot(jnp.concatenate([kk_operand(bi, t), x_ref[bi]], axis=1), bdd)
                    for bi in seqs]
            for bi in seqs:
                x_ref[bi] = advance(bi, t, comb[bi][:, :LANES],
                                    vts[bi][:, half * LANES:(half + 1) * LANES])
            for bi in seqs:
                store_o(bi, jnp.maximum(t - 1, 0), comb[bi][:, LANES:])
        return carry

    def flush():
        for bi in seqs:
            store_o(bi, tt - 1, _dot(x_ref[bi], bd))

    return single_step, reset, two_steps, flush


def _scan_kernel(r_ref, w_ref, k_ref, v_ref, kn_ref, b_ref, s0_ref, mask_ref, bd_ref, bdd_ref,
                 o_ref, sout_ref, s_ref, x_ref, *, bb, tt):
    ti = pl.program_id(1)

    @pl.when(ti == 0)
    def _():
        s_ref[...] = s0_ref[...]

    single_step, reset, two_steps, flush = _scan_ops(
        r_ref, w_ref, k_ref, v_ref, kn_ref, b_ref, mask_ref, bd_ref, bdd_ref, o_ref, s_ref, x_ref,
        bb, tt)
    if tt == 1:
        single_step()
    else:
        reset()
        lax.fori_loop(0, tt // 2, two_steps, 0, unroll=4)
        flush()

    @pl.when(ti == pl.num_programs(1) - 1)
    def _():
        sout_ref[...] = s_ref[...]


def _rwkv_scan(seqs, s0, consts, tt):
    bsz, _, t, _ = seqs[0].shape
    bb = 2
    rows = N_PAIRS * HEAD_DIM
    seq_spec = pl.BlockSpec((bb, N_PAIRS, tt, LANES), lambda bi, ti: (bi, 0, ti, 0))
    st_spec = pl.BlockSpec((bb, rows, LANES), lambda bi, ti: (bi, 0, 0))
    const = lambda shape: pl.BlockSpec(shape, lambda bi, ti: (0,) * len(shape))
    return pl.pallas_call(
        functools.partial(_scan_kernel, bb=bb, tt=tt),
        out_shape=[jax.ShapeDtypeStruct((bsz, N_PAIRS, t, LANES), F32),
                   jax.ShapeDtypeStruct((bsz, rows, LANES), F32)],
        grid=(bsz // bb, t // tt),
        in_specs=[seq_spec] * 6 + [st_spec, const((rows, LANES)), const((LANES, LANES)),
                                   const((2 * LANES, 2 * LANES))],
        out_specs=[seq_spec, st_spec],
        scratch_shapes=[pltpu.VMEM((bb, rows, LANES), F32), pltpu.VMEM((bb, rows, LANES), BF16)],
        compiler_params=_cparams(("parallel", "arbitrary")),
        name="rwkv_scan",
    )(*seqs, s0, *consts)


def _pairs_from_heads(s):
    bsz = s.shape[0]
    s = s.reshape(bsz, N_PAIRS, 2, HEAD_DIM, HEAD_DIM).transpose(0, 1, 3, 2, 4)
    return s.reshape(bsz, N_PAIRS * HEAD_DIM, LANES)


def _heads_from_pairs(s):
    bsz = s.shape[0]
    s = s.reshape(bsz, N_PAIRS, HEAD_DIM, 2, HEAD_DIM).transpose(0, 1, 3, 2, 4)
    return s.reshape(bsz, N_HEADS, HEAD_DIM, HEAD_DIM)


def _post_kernel(o_ref, bo_ref, g_ref, lg_ref, lb_ref, bd_ref, out_ref):
    bd = bd_ref[...]
    o = jnp.concatenate([o_ref[0, p] for p in range(N_PAIRS)], axis=1)
    mean = _seg_sum(o, bd) * (1.0 / HEAD_DIM)
    d = o - mean
    var = _seg_sum(d * d, bd) * (1.0 / HEAD_DIM)
    on = d * lax.rsqrt(var + GN_EPS) * lg_ref[...] + lb_ref[...]
    out_ref[0] = ((on + bo_ref[0]) * g_ref[0]).astype(BF16)


def _rwkv_post(o, bonus, g, lnx_g, lnx_b, bd, tb):
    bsz, _, t, _ = o.shape
    spec = pl.BlockSpec((1, tb, WIDTH), lambda b, i: (b, i, 0))
    o_spec = pl.BlockSpec((1, N_PAIRS, tb, LANES), lambda b, i: (b, 0, i, 0))
    const = lambda shape: pl.BlockSpec(shape, lambda b, i: (0,) * len(shape))
    return pl.pallas_call(
        _post_kernel,
        out_shape=jax.ShapeDtypeStruct((bsz, t, WIDTH), BF16),
        grid=(bsz, t // tb),
        in_specs=[o_spec, spec, spec, const((1, WIDTH)), const((1, WIDTH)), const((LANES, LANES))],
        out_specs=spec,
        compiler_params=_cparams(("parallel", "parallel")),
        name="rwkv_post",
    )(o, bonus, g, lnx_g, lnx_b, bd)


def _fox_norm(x, gain, bd):
    ms = _seg_sum(x * x, bd) * (1.0 / HEAD_DIM)
    return x * lax.rsqrt(ms + NORM_EPS) * gain


def _log_sigmoid(x):
    return jnp.minimum(x, 0.0) - jnp.log(1.0 + jnp.exp(-jnp.abs(x)))


def _fox_prep_kernel(fq_ref, fk_ref, fv_ref, fl_ref, bf_ref, qg_ref, kg_ref, bd_ref, tril_ref,
                     pq_ref, pk_ref, kn_o, v_o, lf_o, qa_o, ka_o, va_o, c_ref):
    bd = bd_ref[...]
    q = _fox_norm(fq_ref[0], qg_ref[...], bd) * ATTN_SCALE
    kn = _fox_norm(fk_ref[0], kg_ref[...], bd)
    v = fv_ref[0]
    kn_o[0] = kn
    v_o[0] = v
    lane = lax.broadcasted_iota(jnp.int32, fl_ref[0].shape, 1)
    lf = jnp.where(lane < N_HEADS, _log_sigmoid(fl_ref[0] + bf_ref[...]), 0.0)
    lf_o[0] = lf

    @pl.when(pl.program_id(1) == 0)
    def _():
        c_ref[...] = jnp.zeros_like(c_ref)

    tril = tril_ref[...]
    h1, h2, h3 = _split3(lf)
    c = _dot(tril, h1) + _dot(tril, h2) + _dot(tril, h3) + c_ref[...]
    n = c.shape[0]
    c_ref[...] = c[n - 1:n, :]
    c1, c2, c3 = _split3(c)
    cpack = (c1.astype(F32) + pltpu.roll(c2.astype(F32), N_HEADS, 1)
             + pltpu.roll(c3.astype(F32), 2 * N_HEADS, 1)).astype(BF16)
    lane_v = lax.broadcasted_iota(jnp.int32, (n, LANES), 1)
    ones_q = jnp.where((lane_v >= HEAD_DIM + 3) & (lane_v < HEAD_DIM + 6), 1.0, 0.0)
    ones_k = jnp.where((lane_v >= HEAD_DIM) & (lane_v < HEAD_DIM + 3), 1.0, 0.0)
    for p in range(N_PAIRS):
        sl = slice(p * LANES, (p + 1) * LANES)
        lhs_q = jnp.concatenate([q[:, sl].astype(BF16), cpack], axis=1)
        lhs_k = jnp.concatenate([kn[:, sl].astype(BF16), cpack], axis=1)
        vs = v[:, sl]
        for hh in range(2):
            h = 2 * p + hh
            qa_o[0, h] = (_dot(lhs_q, pq_ref[h]) + ones_q).astype(BF16)
            ka_o[0, h] = (_dot(lhs_k, pk_ref[h]) + ones_k).astype(BF16)
            if hh == 0:
                va = jnp.where(lane_v < HEAD_DIM, vs, jnp.where(lane_v == HEAD_DIM, 1.0, 0.0))
            else:
                va = jnp.where(lane_v >= HEAD_DIM, vs, jnp.where(lane_v == 0, 1.0, 0.0))
            va_o[0, h] = va.astype(BF16)


def _fox_prep_plain_kernel(fq_ref, fk_ref, fv_ref, fl_ref, bf_ref, qg_ref, kg_ref, bd_ref,
                           kn_o, v_o, lf_o, q_o):
    bd = bd_ref[...]
    q_o[0] = _fox_norm(fq_ref[0], qg_ref[...], bd) * ATTN_SCALE
    kn_o[0] = _fox_norm(fk_ref[0], kg_ref[...], bd)
    v_o[0] = fv_ref[0]
    lane = lax.broadcasted_iota(jnp.int32, fl_ref[0].shape, 1)
    lf_o[0] = jnp.where(lane < N_HEADS, _log_sigmoid(fl_ref[0] + bf_ref[...]), 0.0)


def _fox_prep(z3, params, tb, augment):
    bsz, t, _ = z3.shape
    z_specs = [
        pl.BlockSpec((1, tb, WIDTH), lambda b, i: (b, i, COL_FQ // WIDTH)),
        pl.BlockSpec((1, tb, WIDTH), lambda b, i: (b, i, COL_FK // WIDTH)),
        pl.BlockSpec((1, tb, WIDTH), lambda b, i: (b, i, COL_FV // WIDTH)),
        pl.BlockSpec((1, tb, LANES), lambda b, i: (b, i, COL_FL // LANES)),
    ]
    const = lambda shape: pl.BlockSpec(shape, lambda b, i: (0,) * len(shape))
    w_specs = [const((1, LANES)), const((1, WIDTH)), const((1, WIDTH)), const((LANES, LANES))]
    row_spec = pl.BlockSpec((1, tb, WIDTH), lambda b, i: (b, i, 0))
    lf_spec = pl.BlockSpec((1, tb, LANES), lambda b, i: (b, i, 0))
    row_shape = jax.ShapeDtypeStruct((bsz, t, WIDTH), F32)
    lf_shape = jax.ShapeDtypeStruct((bsz, t, LANES), F32)
    if not augment:
        return pl.pallas_call(
            _fox_prep_plain_kernel,
            out_shape=[row_shape, row_shape, lf_shape, row_shape],
            grid=(bsz, t // tb),
            in_specs=z_specs + w_specs,
            out_specs=[row_spec, row_spec, lf_spec, row_spec],
            compiler_params=_cparams(("parallel", "arbitrary")),
            name="fox_prep_sample",
        )(z3, z3, z3, z3, *params[:4])
    aug_spec = pl.BlockSpec((1, N_HEADS, tb, LANES), lambda b, i: (b, 0, i, 0))
    aug_shape = jax.ShapeDtypeStruct((bsz, N_HEADS, t, LANES), BF16)
    w_specs += [const((tb, tb)), const((N_HEADS, 2 * LANES, LANES)),
                const((N_HEADS, 2 * LANES, LANES))]
    return pl.pallas_call(
        _fox_prep_kernel,
        out_shape=[row_shape, row_shape, lf_shape, aug_shape, aug_shape, aug_shape],
        grid=(bsz, t // tb),
        in_specs=z_specs + w_specs,
        out_specs=[row_spec, row_spec, lf_spec, aug_spec, aug_spec, aug_spec],
        scratch_shapes=[pltpu.VMEM((1, LANES), F32)],
        compiler_params=_cparams(("parallel", "arbitrary")),
        name="fox_prep",
    )(z3, z3, z3, z3, *params)


def _place_matrices():
    h = jnp.arange(N_HEADS)[:, None, None]
    row = jnp.arange(2 * LANES)[None, :, None]
    col = jnp.arange(LANES)[None, None, :]
    data = (row < LANES) & (row == (h % 2) * HEAD_DIM + col) & (col < HEAD_DIM)
    piece = (row - LANES) // N_HEADS
    is_c = (row >= LANES) & (row < LANES + 3 * N_HEADS) & ((row - LANES) % N_HEADS == h)
    cq = is_c & (col == HEAD_DIM + piece)
    ck = is_c & (col == HEAD_DIM + 3 + piece)
    pq = (data | cq).astype(F32)
    pk = data.astype(F32) - ck.astype(F32)
    return pq.astype(BF16), pk.astype(BF16)


def _flash_kernel(q_ref, k_ref, v_ref, og_ref, o_ref, *, tq, tk, nh):
    qi = pl.program_id(2)
    n_full = (qi * tq) // tk
    row = lax.broadcasted_iota(jnp.int32, (tq, tk), 0) + qi * tq
    col = lax.broadcasted_iota(jnp.int32, (tq, tk), 1) + n_full * tk
    causal = col <= row
    qs = [q_ref[0, hh] for hh in range(nh)]

    def block(j, carry, masked):
        start = pl.multiple_of(j * tk, tk)
        new = []
        for hh in range(nh):
            m, acc = carry[2 * hh], carry[2 * hh + 1]
            kb = k_ref[0, hh, pl.ds(start, tk), :]
            vb = v_ref[0, hh, pl.ds(start, tk), :]
            s = lax.dot_general(qs[hh], kb, (((1,), (1,)), ((), ())), preferred_element_type=F32)
            if masked:
                s = jnp.where(causal, s, NEG_BIG)
            m_new = jnp.maximum(m, jnp.max(s, axis=1, keepdims=True))
            p = jnp.exp(s - m_new)
            new += [m_new, jnp.exp(m - m_new) * acc + _dot(p.astype(BF16), vb)]
        return tuple(new)

    init = (jnp.full((tq, 1), NEG_BIG, F32), jnp.zeros((tq, LANES), F32)) * nh
    carry = lax.fori_loop(0, n_full, lambda j, c: block(j, c, False), init)
    carry = block(n_full, carry, True)
    lane = lax.broadcasted_iota(jnp.int32, (tq, LANES), 1)
    for pp in range(nh // 2):
        acc_e, acc_o = carry[4 * pp + 1], carry[4 * pp + 3]
        o = jnp.where(lane < HEAD_DIM, acc_e / acc_e[:, HEAD_DIM:HEAD_DIM + 1],
                      acc_o / acc_o[:, 0:1])
        sl = slice(pp * LANES, (pp + 1) * LANES)
        o_ref[0, :, sl] = (o * _sigmoid(og_ref[0, :, sl])).astype(BF16)


def _fox_flash(qa, ka, va, z3, tq, tk, nh):
    bsz, _, t, _ = qa.shape
    gw = (nh // 2) * LANES
    return pl.pallas_call(
        functools.partial(_flash_kernel, tq=tq, tk=tk, nh=nh),
        out_shape=jax.ShapeDtypeStruct((bsz, t, WIDTH), BF16),
        grid=(bsz, N_HEADS // nh, t // tq),
        in_specs=[
            pl.BlockSpec((1, nh, tq, LANES), lambda b, g, i: (b, g, i, 0)),
            pl.BlockSpec((1, nh, t, LANES), lambda b, g, i: (b, g, 0, 0)),
            pl.BlockSpec((1, nh, t, LANES), lambda b, g, i: (b, g, 0, 0)),
            pl.BlockSpec((1, tq, gw), lambda b, g, i: (b, i, COL_OG // gw + g)),
        ],
        out_specs=pl.BlockSpec((1, tq, gw), lambda b, g, i: (b, i, g)),
        compiler_params=_cparams(("parallel", "parallel", "arbitrary")),
        name="fox_flash",
    )(qa, ka, va, z3)


def _decode_ops(qt_ref, kc_ref, vc_ref, lc_ref, og_ref, lt_ref, on_ref, o_ref,
                m_ref, l_ref, cy_ref, acc_ref):
    later = lt_ref[...]
    ones = on_ref[...]

    def rows(fn):
        return jnp.concatenate([fn(h) for h in range(N_HEADS)], axis=0)

    def init():
        m_ref[...] = rows(lambda h: jnp.sum(qt_ref[0, h] * kc_ref[0, h], axis=0, keepdims=True))
        l_ref[...] = jnp.ones_like(l_ref)
        cy_ref[...] = lc_ref[0]
        lane = lax.broadcasted_iota(jnp.int32, (HEAD_DIM, LANES), 1)
        for h in range(N_HEADS):
            acc_ref[h] = jnp.where(lane == 0, vc_ref[0, h], 0.0)

    def page(lf, k_tile, v_tile):
        h1, h2, h3 = _split3(lf)
        bias = _dot(h1, later) + _dot(h2, later) + _dot(h3, later) + cy_ref[...]
        cy_ref[...] = cy_ref[...] + _dot(h1, ones) + _dot(h2, ones) + _dot(h3, ones)
        s = rows(lambda h: jnp.sum(qt_ref[0, h] * k_tile(h), axis=0, keepdims=True)) + bias
        m_old = m_ref[...]
        m_new = jnp.maximum(m_old, jnp.max(s, axis=1, keepdims=True))
        alpha = jnp.exp(m_old - m_new)
        p = jnp.exp(s - m_new)
        m_ref[...] = m_new
        l_ref[...] = alpha * l_ref[...] + jnp.sum(p, axis=1, keepdims=True)
        for h in range(N_HEADS):
            acc_ref[h] = alpha[h:h + 1, :] * acc_ref[h] + p[h:h + 1, :] * v_tile(h)

    def finish():
        ones8 = ones[0:8, :]

        def token_sum(h):
            nt = lambda a: lax.dot_general(ones8, a, (((1,), (1,)), ((), ())),
                                           preferred_element_type=F32)
            a1, a2, a3 = _split3(acc_ref[h])
            return (nt(a1) + nt(a2) + nt(a3))[0:1, :]

        o_ref[0] = (rows(token_sum) / l_ref[:, 0:HEAD_DIM]) * _sigmoid(og_ref[0])

    return init, page, finish


def _scan_decode_kernel(pt_ref, r_ref, w_ref, k_ref, v_ref, kn_ref, b_ref, s0_ref, mask_ref,
                        bd_ref, bdd_ref, qt_ref, kc_ref, vc_ref, lc_ref, og_ref, lt_ref, on_ref,
                        ck_hbm, cv_hbm, cl_hbm, o_ref, sout_ref, od_ref,
                        s_ref, x_ref, kbuf, vbuf, lbuf, sem, m_ref, l_ref, cy_ref, acc_ref,
                        *, bb, tt, npg, n_pages):
    g = pl.program_id(0)

    @pl.when(g == 0)
    def _():
        s_ref[...] = s0_ref[...]

    _, reset, two_steps, flush = _scan_ops(
        r_ref, w_ref, k_ref, v_ref, kn_ref, b_ref, mask_ref, bd_ref, bdd_ref, o_ref, s_ref, x_ref,
        bb, tt)
    d_init, d_page, d_finish = _decode_ops(qt_ref, kc_ref, vc_ref, lc_ref, og_ref, lt_ref, on_ref,
                                           od_ref, m_ref, l_ref, cy_ref, acc_ref)
    n_chunks = n_pages // npg
    pairs_per_chunk = tt // (2 * n_chunks)

    def chunk_copies(seq, c, slot):
        cps = []
        for i in range(npg):
            pg = pt_ref[seq, n_pages - 1 - (c * npg + i)]
            cps += [pltpu.make_async_copy(ck_hbm.at[pg], kbuf.at[slot, i], sem.at[slot, 0]),
                    pltpu.make_async_copy(cv_hbm.at[pg], vbuf.at[slot, i], sem.at[slot, 1]),
                    pltpu.make_async_copy(cl_hbm.at[pg], lbuf.at[slot, i], sem.at[slot, 2])]
        return cps

    def start_chunk(seq, c, slot):
        for n, cp in enumerate(chunk_copies(seq, c, slot)):
            cp.start(priority=n % 2)

    @pl.when(g == 0)
    def _():
        start_chunk(0, 0, 0)

    reset()
    d_init()

    def chunk(c, carry):
        slot = lax.rem(c, 2)
        for cp in chunk_copies(g, c, slot):
            cp.wait()

        @pl.when(c + 1 < n_chunks)
        def _():
            start_chunk(g, c + 1, 1 - slot)

        @pl.when((c + 1 == n_chunks) & (g + 1 < pl.num_programs(0)))
        def _():
            start_chunk(g + 1, 0, 0)

        for i in range(npg):
            d_page(lbuf[slot, i], lambda h, i=i: kbuf[slot, i, h], lambda h, i=i: vbuf[slot, i, h])
        lax.fori_loop(0, pairs_per_chunk, lambda q, cr: two_steps(c * pairs_per_chunk + q, cr), 0,
                      unroll=min(pairs_per_chunk, 4))
        return carry

    lax.fori_loop(0, n_chunks, chunk, 0)
    flush()
    d_finish()

    @pl.when(g == pl.num_programs(0) - 1)
    def _():
        sout_ref[...] = s_ref[...]


def _rwkv_scan_fox_decode(scan_seqs, s0, scan_consts, page_table, qt, kc, vc, lfc, og,
                          cache_kt, cache_vt, cache_lt, later, ones, npg):
    bsz, _, t, _ = scan_seqs[0].shape
    dbsz, n_pages = page_table.shape
    page = cache_kt.shape[-1]
    rows = N_PAIRS * HEAD_DIM
    tt = t // dbsz
    n_chunks = n_pages // npg
    assert tt * dbsz == t and n_chunks * npg == n_pages and n_chunks % 2 == 0
    assert tt % (2 * n_chunks) == 0
    seq_spec = pl.BlockSpec((bsz, N_PAIRS, tt, LANES), lambda g, pt: (0, 0, g, 0))
    st_spec = pl.BlockSpec((bsz, rows, LANES), lambda g, pt: (0, 0, 0))
    const = lambda shape: pl.BlockSpec(shape, lambda g, pt: (0,) * len(shape))
    tile_spec = pl.BlockSpec((1, N_HEADS, HEAD_DIM, page), lambda g, pt: (g, 0, 0, 0))
    hbm = pl.BlockSpec(memory_space=pl.ANY)
    in_specs = ([seq_spec] * N_SCAN_SEQS
                + [st_spec, const((rows, LANES)), const((LANES, LANES)), const((2 * LANES, 2 * LANES))]
                + [tile_spec] * 3
                + [pl.BlockSpec((1, N_HEADS, page), lambda g, pt: (g, 0, 0)),
                   pl.BlockSpec((1, N_HEADS, HEAD_DIM), lambda g, pt: (g, 0, 0)),
                   const((page, page)), const((page, page)), hbm, hbm, hbm])
    stat = pltpu.VMEM((N_HEADS, page), F32)
    grid_spec = pltpu.PrefetchScalarGridSpec(
        num_scalar_prefetch=1,
        grid=(dbsz,),
        in_specs=in_specs,
        out_specs=[seq_spec, st_spec,
                   pl.BlockSpec((1, N_HEADS, HEAD_DIM), lambda g, pt: (g, 0, 0))],
        scratch_shapes=[pltpu.VMEM((bsz, rows, LANES), F32), pltpu.VMEM((bsz, rows, LANES), BF16),
                        pltpu.VMEM((2, npg, N_HEADS, HEAD_DIM, page), F32),
                        pltpu.VMEM((2, npg, N_HEADS, HEAD_DIM, page), F32),
                        pltpu.VMEM((2, npg, N_HEADS, page), F32),
                        pltpu.SemaphoreType.DMA((2, 3)),
                        stat, stat, stat, pltpu.VMEM((N_HEADS, HEAD_DIM, page), F32)],
    )
    return pl.pallas_call(
        functools.partial(_scan_decode_kernel, bb=bsz, tt=tt, npg=npg, n_pages=n_pages),
        out_shape=[jax.ShapeDtypeStruct((bsz, N_PAIRS, t, LANES), F32),
                   jax.ShapeDtypeStruct((bsz, rows, LANES), F32),
                   jax.ShapeDtypeStruct((dbsz, N_HEADS, HEAD_DIM), F32)],
        grid_spec=grid_spec,
        compiler_params=_cparams(("arbitrary",)),
        name="rwkv_scan_fox_decode",
    )(page_table, *scan_seqs, s0, *scan_consts, qt, kc, vc, lfc, og, later, ones,
      cache_kt, cache_vt, cache_lt)


def _out_proj_kernel(x_ref, oa_ref, ob_ref, wa_ref, wb_ref, h_ref):
    h_ref[...] = x_ref[...] + _dot(oa_ref[...], wa_ref[...]) + _dot(ob_ref[...], wb_ref[...])


def _out_proj(x, oa, ob, w_out, tm, tn):
    m, d = x.shape
    return pl.pallas_call(
        _out_proj_kernel,
        out_shape=jax.ShapeDtypeStruct((m, d), F32),
        grid=(m // tm, d // tn),
        in_specs=[
            pl.BlockSpec((tm, tn), lambda i, j: (i, j)),
            pl.BlockSpec((tm, WIDTH), lambda i, j: (i, 0)),
            pl.BlockSpec((tm, WIDTH), lambda i, j: (i, 0)),
            pl.BlockSpec((WIDTH, tn), lambda i, j: (0, j)),
            pl.BlockSpec((WIDTH, tn), lambda i, j: (1, j)),
        ],
        out_specs=pl.BlockSpec((tm, tn), lambda i, j: (i, j)),
        compiler_params=_cparams(("parallel", "arbitrary")),
        name="out_proj",
    )(x, oa, ob, w_out, w_out)


def _ffn_kernel(h_ref, gn_ref, wg_ref, wu_ref, wd_ref, gf_ref, y_ref, hn_ref, acc_ref):
    f = pl.program_id(1)

    @pl.when(f == 0)
    def _():
        h = h_ref[...]
        ms = jnp.mean(h * h, axis=-1, keepdims=True)
        hn_ref[...] = (h * lax.rsqrt(ms + NORM_EPS) * gn_ref[...]).astype(BF16)
        acc_ref[...] = jnp.zeros_like(acc_ref)

    hn = hn_ref[...]
    gate = _dot(hn, wg_ref[...])
    up = _dot(hn, wu_ref[...])
    act = (gate * _sigmoid(gate) * up).astype(BF16)
    acc_ref[...] += _dot(act, wd_ref[...])

    @pl.when(f == pl.num_programs(1) - 1)
    def _():
        hh = h_ref[...] + acc_ref[...]
        ms = jnp.mean(hh * hh, axis=-1, keepdims=True)
        y_ref[...] = hh * lax.rsqrt(ms + NORM_EPS) * gf_ref[...]


def _ffn(h, g_ffn, wg, wu, wd, g_final, tm, tf):
    m, d = h.shape
    dff = wg.shape[1]
    return pl.pallas_call(
        _ffn_kernel,
        out_shape=jax.ShapeDtypeStruct((m, d), F32),
        grid=(m // tm, dff // tf),
        in_specs=[
            pl.BlockSpec((tm, d), lambda i, f: (i, 0)),
            pl.BlockSpec((1, d), lambda i, f: (0, 0)),
            pl.BlockSpec((d, tf), lambda i, f: (0, f)),
            pl.BlockSpec((d, tf), lambda i, f: (0, f)),
            pl.BlockSpec((tf, d), lambda i, f: (f, 0)),
            pl.BlockSpec((1, d), lambda i, f: (0, 0)),
        ],
        out_specs=pl.BlockSpec((tm, d), lambda i, f: (i, 0)),
        scratch_shapes=[pltpu.VMEM((tm, d), BF16), pltpu.VMEM((tm, d), F32)],
        compiler_params=_cparams(("parallel", "arbitrary")),
        name="ffn",
    )(h, g_ffn, wg, wu, wd, g_final)


def _block_diag_ones():
    i = jnp.arange(LANES)
    return (i[:, None] // HEAD_DIM == i[None, :] // HEAD_DIM).astype(BF16)


def _pick(n, cap):
    t = min(n, cap)
    while n % t:
        t //= 2
    return t


def kernel(x_prompt, x_sample, cache_k, cache_v, cache_logf, state_wkv, state_shift, page_table,
           norm_mix, w_in, mu_shift, w0, w_decay_up, a0, w_aaa_up, w_gate_up, k_k, k_a, r_k,
           lnx_g, lnx_b, b_f, q_norm, k_norm, w_out, norm_ffn, w_ffn_gate, w_ffn_up, w_ffn_down,
           norm_final):
    bsz, seq, d = x_prompt.shape
    dbsz = x_sample.shape[0]
    lyr = 0
    rw = RWKV_COLS

    wi = w_in[lyr]
    w_z = jnp.concatenate(
        [wi[:, rw:rw + 4 * WIDTH], wi[:, :rw], wi[:, rw + 4 * WIDTH:],
         jnp.zeros((d, Z_COLS - N_Z_USED), F32)], axis=1).astype(BF16)
    g_mix = norm_mix[lyr][None, :]
    wda = jnp.zeros((LANES, 2 * WIDTH), F32)
    wda = wda.at[:HEAD_DIM, :WIDTH].set(w_decay_up[lyr]).at[HEAD_DIM:, WIDTH:].set(w_aaa_up[lyr])
    bd = _block_diag_ones()
    pre_params = (mu_shift[lyr][None, :], w0[lyr][None, :], a0[lyr][None, :], k_k[lyr][None, :],
                  k_a[lyr][None, :], r_k[lyr].reshape(1, WIDTH), wda.astype(BF16),
                  w_gate_up[lyr].astype(BF16), bd)
    li = jnp.arange(LANES)
    maskf = (li[None, :] % HEAD_DIM
             == jnp.arange(N_PAIRS * HEAD_DIM)[:, None] % HEAD_DIM).astype(F32)
    zb = jnp.zeros_like(bd)
    bdd = jnp.concatenate([jnp.concatenate([bd, zb], axis=1),
                           jnp.concatenate([zb, bd], axis=1)], axis=0)
    scan_consts = (maskf, bd, bdd)
    lg, lb = lnx_g[lyr][None, :], lnx_b[lyr][None, :]
    bfp = jnp.zeros((1, LANES), F32).at[0, :N_HEADS].set(b_f[lyr])
    qg = jnp.tile(q_norm[lyr], N_HEADS)[None, :]
    kg = jnp.tile(k_norm[lyr], N_HEADS)[None, :]
    tb_fox = _pick(seq, 256)
    tril = (jnp.arange(tb_fox)[:, None] >= jnp.arange(tb_fox)[None, :]).astype(BF16)
    pq, pk = _place_matrices()
    w_o = w_out[lyr].astype(BF16)
    wg = w_ffn_gate[lyr].astype(BF16)
    wu = w_ffn_up[lyr].astype(BF16)
    wd = w_ffn_down[lyr].astype(BF16)
    g_ffn = norm_ffn[lyr][None, :]
    g_fin = norm_final[None, :]

    m_p = bsz * seq
    xp = x_prompt.reshape(m_p, d)
    z_p = _norm_matmul(xp, g_mix, w_z, _pick(m_p, 1024), 1536)
    z3p = z_p.reshape(bsz, seq, Z_COLS)
    first = jnp.zeros((bsz, 1, rw), F32)
    pre_p = _rwkv_prelude(z3p, first, pre_params, _pick(seq, 256), True)
    bonus_p, gate_p = pre_p[N_SCAN_SEQS:]

    xs = x_sample.reshape(dbsz, d)
    z_s = _norm_matmul(xs, g_mix, w_z, dbsz, 512)
    z3s = z_s.reshape(1, dbsz, Z_COLS)
    prev_s = state_shift[lyr].reshape(1, dbsz, rw)
    pre_s = _rwkv_prelude(z3s, prev_s, pre_params, dbsz, False)
    bonus_s, gate_s = pre_s[N_SCAN_SEQS:]
    fkn_s, fv_s, lf_s, q_s = _fox_prep(z3s, (bfp, qg, kg, bd), dbsz, False)

    page = cache_k.shape[2]
    lane_tile = lambda a: jnp.broadcast_to(a.reshape(dbsz, N_HEADS, HEAD_DIM, 1),
                                           (dbsz, N_HEADS, HEAD_DIM, page))
    lfc = jnp.broadcast_to(lf_s[0, :, :N_HEADS, None], (dbsz, N_HEADS, page))
    og_s = z_s[:, COL_OG:COL_OG + WIDTH].reshape(dbsz, N_HEADS, HEAD_DIM)
    later = (jnp.arange(page)[:, None] > jnp.arange(page)[None, :]).astype(BF16)
    s0 = jnp.zeros((bsz, N_PAIRS * HEAD_DIM, LANES), F32)
    o_p, s_p, ob_s = _rwkv_scan_fox_decode(
        pre_p[:N_SCAN_SEQS], s0, scan_consts, page_table, lane_tile(q_s), lane_tile(fkn_s),
        lane_tile(fv_s), lfc, og_s, cache_k[lyr].transpose(0, 2, 3, 1),
        cache_v[lyr].transpose(0, 2, 3, 1), cache_logf[lyr].transpose(0, 2, 1), later,
        jnp.ones((page, page), BF16), _pick(page_table.shape[1], 8))

    oa_p = _rwkv_post(o_p, bonus_p, gate_p, lg, lb, bd, _pick(seq, 512))
    fkn_p, fv_p, lf_p, qa, ka, va = _fox_prep(z3p, (bfp, qg, kg, bd, tril, pq, pk), tb_fox, True)
    ob_p = _fox_flash(qa, ka, va, z3p, _pick(seq, 1024), _pick(seq, 1024), 2)
    h_p = _out_proj(xp, oa_p.reshape(m_p, WIDTH), ob_p.reshape(m_p, WIDTH), w_o,
                    _pick(m_p, 1024), 512)
    y_p = _ffn(h_p, g_ffn, wg, wu, wd, g_fin, _pick(m_p, 512), 512)

    seq_major = lambda a: a.transpose(2, 1, 0, 3)
    o_s, s_s = _rwkv_scan([seq_major(a) for a in pre_s[:N_SCAN_SEQS]],
                          _pairs_from_heads(state_wkv[lyr]), scan_consts, 1)
    oa_s = _rwkv_post(seq_major(o_s), bonus_s, gate_s, lg, lb, bd, dbsz)
    h_s = _out_proj(xs, oa_s.reshape(dbsz, WIDTH), ob_s.reshape(dbsz, WIDTH).astype(BF16),
                    w_o, dbsz, 512)
    y_s = _ffn(h_s, g_ffn, wg, wu, wd, g_fin, dbsz, 512)

    hd = (N_HEADS, HEAD_DIM)
    return (
        y_p.reshape(bsz, seq, d),
        y_s.reshape(dbsz, 1, d),
        fkn_p.reshape(1, bsz, seq, *hd),
        fv_p.reshape(1, bsz, seq, *hd),
        lf_p[:, :, :N_HEADS].reshape(1, bsz, seq, N_HEADS),
        _heads_from_pairs(s_p)[None],
        z3p[:, seq - 1, COL_R:COL_R + rw][None],
        fkn_s.reshape(1, dbsz, 1, *hd),
        fv_s.reshape(1, dbsz, 1, *hd),
        lf_s[0, :, :N_HEADS].reshape(1, dbsz, 1, N_HEADS),
        _heads_from_pairs(s_s)[None],
        z_s[:, COL_R:COL_R + rw][None],
    )
```
